```python
import jax, jax.numpy as jnp
from jax import lax
import numpy as np

D_MODEL = 1024
BATCH = 32
SEQ = 256
DEPTH = 4
DEC_BATCH = 8
DEC_SEQ = 4096
PAST_LEN = 512

GRID_W = 64
N_MIXERS = 2
N_ATTN = (DEPTH + 1) // 2
N_RWKV = DEPTH // 2
HEAD_DIM = 64
N_HEADS = D_MODEL // HEAD_DIM
N_KV = N_HEADS // 4
GQA_GROUP = N_HEADS // N_KV
QKV_DIM = (N_HEADS + 2 * N_KV) * HEAD_DIM
WINDOW = 128
BLK = 128
ROPE_BASE = 10000.0
ROPE_PAIRS = HEAD_DIM // 4
RWKV_HEAD = 64
RWKV_H = D_MODEL // RWKV_HEAD
DECAY_LORA = 64
ICLR_LORA = 64
GATE_LORA = 128
D_FF = 2816
N_ADA = 9
RMS_EPS = 1e-6
GN_EPS = 64e-5
NEG_INF = -1e30

kernel_name = 'hybrid_swa_rwkv7_dit_step'


def rmsnorm(x, g):
    xf = x.astype(jnp.float32)
    y = xf * lax.rsqrt(jnp.mean(xf * xf, axis=-1, keepdims=True) + RMS_EPS)
    return (y * g.astype(jnp.float32)).astype(x.dtype)


def sub_in(x, g, shift, scale):
    return rmsnorm(x, g) * (1 + scale) + shift


def ada_params(cond, w, b):
    m = jax.nn.silu(cond) @ w + b
    return jnp.split(m[:, None, :], N_ADA, axis=-1)


def swiglu(h, w_in, w_out):
    gate, up = jnp.split(h @ w_in, 2, axis=-1)
    return (jax.nn.silu(gate) * up) @ w_out


def axial_rope_tables(L):
    rows = L // GRID_W
    row = jnp.repeat(jnp.arange(rows, dtype=jnp.float32), GRID_W)
    col = jnp.tile(jnp.arange(GRID_W, dtype=jnp.float32), rows)
    inv = jnp.power(ROPE_BASE, -jnp.arange(ROPE_PAIRS, dtype=jnp.float32) / ROPE_PAIRS)
    ang = jnp.stack([row[:, None] * inv, col[:, None] * inv], axis=1)
    return jnp.cos(ang), jnp.sin(ang)


def apply_rope(x, cos, sin):
    B, L, Hh, _ = x.shape
    xr = x.astype(jnp.float32).reshape(B, L, Hh, 2, 2, ROPE_PAIRS)
    a, b = xr[..., 0, :], xr[..., 1, :]
    c = cos[None, :, None]
    s = sin[None, :, None]
    out = jnp.stack([a * c - b * s, a * s + b * c], axis=-2)
    return out.reshape(x.shape).astype(x.dtype)


def qkv_proj(h, w_qkv):
    B, L, _ = h.shape
    qkv = h @ w_qkv
    q = qkv[..., :N_HEADS * HEAD_DIM].reshape(B, L, N_HEADS, HEAD_DIM)
    k = qkv[..., N_HEADS * HEAD_DIM:(N_HEADS + N_KV) * HEAD_DIM].reshape(B, L, N_KV, HEAD_DIM)
    v = qkv[..., (N_HEADS + N_KV) * HEAD_DIM:].reshape(B, L, N_KV, HEAD_DIM)
    return q, k, v


def sink_softmax(s, sink_g):
    sk = jnp.broadcast_to(sink_g.astype(jnp.float32)[None, :, :, None, None], s.shape[:-1] + (1,))
    p = jax.nn.softmax(jnp.concatenate([s, sk], axis=-1), axis=-1)
    return p[..., :-1]


def attn_context(h, w_qkv, w_o, sink):
    B, S, _ = h.shape
    q, k, v = qkv_proj(h, w_qkv)
    qb = q.reshape(B, S // BLK, BLK, N_KV, GQA_GROUP, HEAD_DIM)
    sink_g = sink.reshape(N_KV, GQA_GROUP)
    scale = HEAD_DIM ** -0.5

    def block(qj):
        s = jnp.einsum('bqkgd,bskd->bkgqs', qj, k).astype(jnp.float32) * scale
        p = sink_softmax(s, sink_g).astype(v.dtype)
        return jnp.einsum('bkgqs,bskd->bqkgd', p, v)

    o = lax.map(block, jnp.moveaxis(qb, 1, 0))
    o = jnp.moveaxis(o, 0, 1).reshape(B, S, D_MODEL)
    return o @ w_o, k, v


def attn_latent(h, k_ctx, v_ctx, w_qkv, w_o, sink, cos, sin):
    B, L, _ = h.shape
    nblk = L // BLK
    q, k, v = qkv_proj(h, w_qkv)
    q = apply_rope(q, cos, sin)
    k = apply_rope(k, cos, sin)
    qb = q.reshape(B, nblk, BLK, N_KV, GQA_GROUP, HEAD_DIM)
    pad = ((0, 0), (BLK, BLK), (0, 0), (0, 0))
    kp = jnp.pad(k, pad)
    vp = jnp.pad(v, pad)
    sink_g = sink.reshape(N_KV, GQA_GROUP)
    scale = HEAD_DIM ** -0.5
    q_off = jnp.arange(BLK)
    k_off = jnp.arange(3 * BLK)

    def block(j):
        qj = lax.dynamic_index_in_dim(qb, j, axis=1, keepdims=False)
        kj = lax.dynamic_slice_in_dim(kp, j * BLK, 3 * BLK, axis=1)
        vj = lax.dynamic_slice_in_dim(vp, j * BLK, 3 * BLK, axis=1)
        qpos = j * BLK + q_off
        kpos = (j - 1) * BLK + k_off
        valid = (jnp.abs(qpos[:, None] - kpos[None, :]) <= WINDOW) & (kpos >= 0)[None, :] & (kpos < L)[None, :]
        s_loc = jnp.einsum('bqkgd,bskd->bkgqs', qj, kj).astype(jnp.float32) * scale
        s_loc = jnp.where(valid, s_loc, NEG_INF)
        s_ctx = jnp.einsum('bqkgd,bckd->bkgqc', qj, k_ctx).astype(jnp.float32) * scale
        p = sink_softmax(jnp.concatenate([s_loc, s_ctx], axis=-1), sink_g).astype(v.dtype)
        return (jnp.einsum('bkgqs,bskd->bqkgd', p[..., :3 * BLK], vj)
                + jnp.einsum('bkgqc,bckd->bqkgd', p[..., 3 * BLK:], v_ctx))

    o = lax.map(block, jnp.arange(nblk))
    o = jnp.moveaxis(o, 0, 1).reshape(B, L, D_MODEL)
    return o @ w_o


def wkv_scan(r, w, k, v, a, b, S0, reverse):
    def step(S, inp):
        rt, wt, kt, vt, at, bt = inp
        sa = jnp.einsum('bhij,bhj->bhi', S, at)
        S = S * wt[:, :, None, :] + sa[..., None] * bt[:, :, None, :] + vt[..., None] * kt[:, :, None, :]
        return S, jnp.einsum('bhij,bhj->bhi', S, rt)

    xs = tuple(jnp.moveaxis(t, 1, 0) for t in (r, w, k, v, a, b))
    S, y = lax.scan(step, S0, xs, reverse=reverse)
    return jnp.moveaxis(y, 0, 1), S


def rwkv_mix(h, S0_f, S0_b, mu, w_rkv, w0, w1, w2, a0, a1, a2, g1, g2, k_k, k_a, r_k, ln_g, ln_b, w_o):
    B, L, _ = h.shape
    f32 = jnp.float32
    prev = jnp.pad(h[:, :-1], ((0, 0), (1, 0), (0, 0)))
    nxt = jnp.pad(h[:, 1:], ((0, 0), (0, 1), (0, 0)))
    xx = 0.5 * (prev + nxt) - h
    xr, xw, xk, xv, xa, xg = (h + xx * mu[i] for i in range(6))
    r = xr @ w_rkv[0]
    k = xk @ w_rkv[1]
    v = xv @ w_rkv[2]

    def heads(t):
        return t.astype(f32).reshape(B, L, RWKV_H, RWKV_HEAD)

    rh = heads(r)
    vh = heads(v)
    kk = heads(k * k_k)
    kk = kk * lax.rsqrt(jnp.sum(kk * kk, axis=-1, keepdims=True) + 1e-12)

    def direction(d, S0, rev):
        w_log = -jax.nn.softplus(-(w0[d] + jnp.tanh(xw @ w1[d]) @ w2[d]).astype(f32)) - 0.5
        decay = jnp.exp(-jnp.exp(w_log))
        a = jax.nn.sigmoid(a0[d] + (xa @ a1[d]) @ a2[d])
        g = jax.nn.sigmoid(xg @ g1[d]) @ g2[d]
        kd = heads(k * (1 + (a - 1) * k_a))
        y, S = wkv_scan(rh, heads(decay), kd, vh, -kk, kk * heads(a), S0.astype(f32), rev)
        mean = jnp.mean(y, axis=-1, keepdims=True)
        var = jnp.mean(jnp.square(y - mean), axis=-1, keepdims=True)
        y = ((y - mean) * lax.rsqrt(var + GN_EPS)).reshape(B, L, D_MODEL) * ln_g[d] + ln_b[d]
        bonus = jnp.sum(rh * kd * r_k[d].astype(f32), axis=-1, keepdims=True) * vh
        y = y + bonus.reshape(B, L, D_MODEL)
        return y.astype(h.dtype) * g, S

    y_f, S_f = direction(0, S0_f, False)
    y_b, S_b = direction(1, S0_b, True)
    return (y_f + y_b) @ w_o, S_f, S_b


def setup_inputs(seed: int = 0) -> dict:
    key = jax.random.key(seed)
    ks = jax.random.split(key, 32)
    f32 = jnp.float32
    D = D_MODEL

    def nrm(k, shape, scale):
        return jax.random.normal(k, shape, f32) * scale

    return {
        'x_prompt': nrm(ks[0], (BATCH, SEQ, D), 1.0),
        'x_sample': nrm(ks[1], (DEC_BATCH, DEC_SEQ, D), 1.0),
        'cache_k': nrm(ks[2], (DEC_BATCH, N_ATTN, PAST_LEN, N_KV, HEAD_DIM), 1.0),
        'cache_v': nrm(ks[3], (DEC_BATCH, N_ATTN, PAST_LEN, N_KV, HEAD_DIM), 1.0),
        'state_wkv': nrm(ks[4], (DEC_BATCH, N_RWKV, 2, RWKV_H, RWKV_HEAD, RWKV_HEAD), 1.0),
        'c': nrm(ks[5], (DEC_BATCH, D), 1.0),
        'c_ctx': nrm(ks[6], (D,), 1.0),
        'ada_w': nrm(ks[7], (DEPTH, D, N_ADA * D), 0.5 * D ** -0.5),
        'ada_b': nrm(ks[8], (DEPTH, N_ADA * D), 0.02),
        'norm_g': 1.0 + nrm(ks[9], (DEPTH, 3, D), 0.02),
        'ffn_w_in': nrm(ks[10], (DEPTH, 2, D, 2 * D_FF), D ** -0.5),
        'ffn_w_out': nrm(ks[11], (DEPTH, 2, D_FF, D), D_FF ** -0.5),
        'attn_w_qkv': nrm(ks[12], (N_ATTN, D, QKV_DIM), D ** -0.5),
        'attn_w_o': nrm(ks[13], (N_ATTN, D, D), D ** -0.5),
        'attn_sink': nrm(ks[14], (N_ATTN, N_HEADS), 1.0),
        'rwkv_mu': jax.random.uniform(ks[15], (N_RWKV, 6, D), f32),
        'rwkv_w_rkv': nrm(ks[16], (N_RWKV, 3, D, D), D ** -0.5),
        'rwkv_w0': jax.random.uniform(ks[17], (N_RWKV, 2, D), f32, -4.0, 0.0),
        'rwkv_w1': nrm(ks[18], (N_RWKV, 2, D, DECAY_LORA), D ** -0.5),
        'rwkv_w2': nrm(ks[19], (N_RWKV, 2, DECAY_LORA, D), 0.5 * DECAY_LORA ** -0.5),
        'rwkv_a0': nrm(ks[20], (N_RWKV, 2, D), 0.1),
        'rwkv_a1': nrm(ks[21], (N_RWKV, 2, D, ICLR_LORA), D ** -0.5),
        'rwkv_a2': nrm(ks[22], (N_RWKV, 2, ICLR_LORA, D), ICLR_LORA ** -0.5),
        'rwkv_g1': nrm(ks[23], (N_RWKV, 2, D, GATE_LORA), D ** -0.5),
        'rwkv_g2': nrm(ks[24], (N_RWKV, 2, GATE_LORA, D), GATE_LORA ** -0.5),
        'rwkv_k_k': 0.85 + nrm(ks[25], (N_RWKV, D), 0.02),
        'rwkv_k_a': 1.0 + nrm(ks[26], (N_RWKV, D), 0.02),
        'rwkv_r_k': nrm(ks[27], (N_RWKV, 2, RWKV_H, RWKV_HEAD), 0.1),
        'rwkv_ln_g': 1.0 + nrm(ks[28], (N_RWKV, 2, D), 0.02),
        'rwkv_ln_b': nrm(ks[29], (N_RWKV, 2, D), 0.02),
        'rwkv_w_o': nrm(ks[30], (N_RWKV, D, D), D ** -0.5),
        'norm_f': 1.0 + nrm(ks[31], (D,), 0.02),
    }


def reference(x_prompt, x_sample, cache_k, cache_v, state_wkv, c, c_ctx,
              ada_w, ada_b, norm_g, ffn_w_in, ffn_w_out,
              attn_w_qkv, attn_w_o, attn_sink,
              rwkv_mu, rwkv_w_rkv, rwkv_w0, rwkv_w1, rwkv_w2, rwkv_a0, rwkv_a1, rwkv_a2,
              rwkv_g1, rwkv_g2, rwkv_k_k, rwkv_k_a, rwkv_r_k, rwkv_ln_g, rwkv_ln_b, rwkv_w_o,
              norm_f):
    cos, sin = axial_rope_tables(x_sample.shape[1])
    xp = x_prompt
    xs = x_sample
    ctx_k, ctx_v, ctx_state = [], [], []
    for l in range(DEPTH):
        mp = ada_params(c_ctx[None, :], ada_w[l], ada_b[l])
        ms = ada_params(c, ada_w[l], ada_b[l])
        xp = xp + 0.5 * mp[2] * swiglu(sub_in(xp, norm_g[l, 0], mp[0], mp[1]), ffn_w_in[l, 0], ffn_w_out[l, 0])
        xs = xs + 0.5 * ms[2] * swiglu(sub_in(xs, norm_g[l, 0], ms[0], ms[1]), ffn_w_in[l, 0], ffn_w_out[l, 0])
        hp = sub_in(xp, norm_g[l, 1], mp[3], mp[4])
        hs = sub_in(xs, norm_g[l, 1], ms[3], ms[4])
        i = l // N_MIXERS
        if l % N_MIXERS == 0:
            op, kc, vc = attn_context(hp, attn_w_qkv[i], attn_w_o[i], attn_sink[i])
            ctx_k.append(kc)
            ctx_v.append(vc)
            os_ = attn_latent(hs, cache_k[:, i], cache_v[:, i], attn_w_qkv[i], attn_w_o[i], attn_sink[i], cos, sin)
        else:
            rw = (rwkv_mu[i], rwkv_w_rkv[i], rwkv_w0[i], rwkv_w1[i], rwkv_w2[i], rwkv_a0[i], rwkv_a1[i],
                  rwkv_a2[i], rwkv_g1[i], rwkv_g2[i], rwkv_k_k[i], rwkv_k_a[i], rwkv_r_k[i],
                  rwkv_ln_g[i], rwkv_ln_b[i], rwkv_w_o[i])
            z = jnp.zeros((xp.shape[0], RWKV_H, RWKV_HEAD, RWKV_HEAD), jnp.float32)
            op, s_f, s_b = rwkv_mix(hp, z, z, *rw)
            ctx_state.append(jnp.stack([s_f, s_b], axis=1).astype(x_prompt.dtype))
            os_, _, _ = rwkv_mix(hs, state_wkv[:, i, 0], state_wkv[:, i, 1], *rw)
        xp = xp + mp[5] * op
        xs = xs + ms[5] * os_
        xp = xp + 0.5 * mp[8] * swiglu(sub_in(xp, norm_g[l, 2], mp[6], mp[7]), ffn_w_in[l, 1], ffn_w_out[l, 1])
        xs = xs + 0.5 * ms[8] * swiglu(sub_in(xs, norm_g[l, 2], ms[6], ms[7]), ffn_w_in[l, 1], ffn_w_out[l, 1])
    y_prompt = rmsnorm(xp, norm_f)
    y_sample = rmsnorm(xs, norm_f)
    new_cache_k = jnp.stack(ctx_k, axis=1)
    new_cache_v = jnp.stack(ctx_v, axis=1)
    new_state_wkv = jnp.stack(ctx_state, axis=1)
    return (y_prompt, y_sample, new_cache_k, new_cache_v, new_state_wkv)
```

```python
import functools

import jax
import jax.numpy as jnp
from jax import lax
from jax.experimental import pallas as pl
from jax.experimental.pallas import tpu as pltpu

F32 = jnp.float32
BF16 = jnp.bfloat16

HEAD_DIM = 64
GQA_GROUP = 4
LANES = 128
N_ADA = 9
ATTN_BLK = 128
GRID_W = 64
ROPE_BASE = 10000.0
ROPE_PAIRS = HEAD_DIM // 4
RMS_EPS = 1e-6
GN_EPS = 64e-5
KK_EPS = 1e-12
NEG_INF = -1e30
SCAN_CHUNK = 64
VMEM_LIMIT = 56 * 1024 * 1024


def _params(*sem):
    return pltpu.CompilerParams(dimension_semantics=sem, vmem_limit_bytes=VMEM_LIMIT)


def _const_spec(shape):
    nd = len(shape)
    return pl.BlockSpec(shape, lambda *_: (0,) * nd, pipeline_mode=pl.Buffered(1))


def _mod_block(d_model, chunk, per_batch):
    def index(b, *_):
        return (b if per_batch else 0, chunk, 0, 0)
    return pl.BlockSpec((None, None, 1, d_model), index)


def _mod_norm(x, g, shift, scale):
    ms = jnp.mean(x * x, axis=-1, keepdims=True)
    return (x * lax.rsqrt(ms + RMS_EPS) * g) * (1.0 + scale) + shift


def _dot(a, b):
    return jnp.dot(a, b, preferred_element_type=F32)


def _dot_nt(a, b):
    return lax.dot_general(a, b, (((1,), (1,)), ((), ())), preferred_element_type=F32)


def _dot_tn(a, b):
    return lax.dot_general(a, b, (((0,), (0,)), ((), ())), preferred_element_type=F32)


def _split(a):
    hi = a.astype(BF16)
    lo = (a - hi.astype(F32)).astype(BF16)
    return hi, lo


def _dot3(a, b, dot=_dot):
    ah, al = _split(a)
    bh, bl = _split(b)
    return dot(ah, bh) + (dot(ah, bl) + dot(al, bh))


def _dot2_exact_rhs(a, b_bf16):
    ah, al = _split(a)
    return _dot(ah, b_bf16) + _dot(al, b_bf16)


def _head_ones(scale):
    r = lax.broadcasted_iota(jnp.int32, (LANES, LANES), 0) // HEAD_DIM
    c = lax.broadcasted_iota(jnp.int32, (LANES, LANES), 1) // HEAD_DIM
    return jnp.where(r == c, scale, 0.0).astype(BF16)


def _ada_body(c_ref, w_ref, b_ref, o_ref):
    c = c_ref[...]
    s = (c * jax.nn.sigmoid(c)).astype(BF16)
    o_ref[...] = _dot(s, w_ref[...].astype(BF16)) + b_ref[...]


def _ada(cond, ada_w, ada_b):
    depth, d_model, n_out = ada_w.shape
    rows = cond.shape[0]
    tn = d_model
    return pl.pallas_call(
        _ada_body,
        out_shape=jax.ShapeDtypeStruct((depth, rows, n_out), F32),
        grid=(depth, n_out // tn),
        in_specs=[
            pl.BlockSpec((rows, d_model), lambda l, n: (0, 0)),
            pl.BlockSpec((None, d_model, tn), lambda l, n: (l, 0, n)),
            pl.BlockSpec((None, 1, tn), lambda l, n: (l, 0, n)),
        ],
        out_specs=pl.BlockSpec((None, rows, tn), lambda l, n: (l, 0, n)),
        compiler_params=_params("parallel", "parallel"),
        name="ada",
    )(cond, ada_w, ada_b.reshape(depth, 1, n_out))


def _ffn_body(x_ref, sh_ref, sc_ref, gt_ref, g_ref, win_ref, wout_ref, o_ref):
    x = x_ref[...]
    h = _mod_norm(x, g_ref[...], sh_ref[...], sc_ref[...]).astype(BF16)
    hh = _dot(h, win_ref[...])
    d_ff = hh.shape[1] // 2
    gate = hh[:, :d_ff]
    act = (gate * jax.nn.sigmoid(gate) * hh[:, d_ff:]).astype(BF16)
    y = _dot(act, wout_ref[...])
    o_ref[...] = x + (0.5 * gt_ref[...]) * y


def _ffn(x, mod, chunks, per_batch, g, w_in, w_out, tm):
    b, l, d = x.shape
    tok = pl.BlockSpec((None, tm, d), lambda i, t: (i, t, 0))
    return pl.pallas_call(
        _ffn_body,
        out_shape=jax.ShapeDtypeStruct(x.shape, F32),
        grid=(b, l // tm),
        in_specs=[
            tok,
            _mod_block(d, chunks[0], per_batch),
            _mod_block(d, chunks[1], per_batch),
            _mod_block(d, chunks[2], per_batch),
            _const_spec((1, d)),
            _const_spec(w_in.shape),
            _const_spec(w_out.shape),
        ],
        out_specs=tok,
        compiler_params=_params("parallel", "parallel"),
        name="ffn",
    )(x, mod, mod, mod, g.reshape(1, d), w_in, w_out)


def _rope(x, cos, sin_signed):
    width = x.shape[1]
    lane = lax.broadcasted_iota(jnp.int32, x.shape, 1)
    first_half = (lane % (2 * ROPE_PAIRS)) < ROPE_PAIRS
    partner = jnp.where(first_half,
                        pltpu.roll(x, width - ROPE_PAIRS, 1),
                        pltpu.roll(x, ROPE_PAIRS, 1))
    return x * cos + partner * sin_signed


def _qkv_body(*refs, rope, d_model, kv_dim):
    if rope:
        x_ref, sh_ref, sc_ref, g_ref, w_ref, cos_ref, sin_ref, q_ref, k_ref, v_ref = refs
    else:
        x_ref, sh_ref, sc_ref, g_ref, w_ref, q_ref, k_ref, v_ref = refs
    h = _mod_norm(x_ref[...], g_ref[...], sh_ref[...], sc_ref[...]).astype(BF16)
    qkv = _dot(h, w_ref[...])
    q = qkv[:, :d_model]
    k = qkv[:, d_model:d_model + kv_dim]
    v = qkv[:, d_model + kv_dim:]
    if rope:
        cos = cos_ref[...]
        sin = sin_ref[...]
        q = _rope(q, jnp.concatenate([cos] * (d_model // LANES), axis=1),
                  jnp.concatenate([sin] * (d_model // LANES), axis=1))
        k = _rope(k, jnp.concatenate([cos] * (kv_dim // LANES), axis=1),
                  jnp.concatenate([sin] * (kv_dim // LANES), axis=1))
    q_ref[...] = (q * (HEAD_DIM ** -0.5)).astype(BF16)
    k_ref[...] = k
    v_ref[...] = v


def _qkv(x, mod, per_batch, g, w_qkv, rope_tables, tm):
    b, l, d = x.shape
    kv_dim = (w_qkv.shape[1] - d) // 2
    rope = rope_tables is not None
    tok = lambda w: pl.BlockSpec((None, tm, w), lambda i, t: (i, t, 0))
    in_specs = [tok(d), _mod_block(d, 3, per_batch), _mod_block(d, 4, per_batch),
                _const_spec((1, d)), _const_spec(w_qkv.shape)]
    args = [x, mod, mod, g.reshape(1, d), w_qkv]
    if rope:
        tab = pl.BlockSpec((tm, LANES), lambda i, t: (t, 0))
        in_specs += [tab, tab]
        args += list(rope_tables)
    return pl.pallas_call(
        functools.partial(_qkv_body, rope=rope, d_model=d, kv_dim=kv_dim),
        out_shape=(jax.ShapeDtypeStruct((b, l, d), BF16),
                   jax.ShapeDtypeStruct((b, l, kv_dim), F32),
                   jax.ShapeDtypeStruct((b, l, kv_dim), F32)),
        grid=(b, l // tm),
        in_specs=in_specs,
        out_specs=(tok(d), tok(kv_dim), tok(kv_dim)),
        compiler_params=_params("parallel", "parallel"),
        name="qkv_rope" if rope else "qkv",
    )(*args)


def _rope_tables(l, grid_w):
    pos = jnp.arange(l, dtype=jnp.int32)
    row = (pos // grid_w).astype(F32)
    col = (pos % grid_w).astype(F32)
    inv = jnp.power(ROPE_BASE, -jnp.arange(ROPE_PAIRS, dtype=F32) / ROPE_PAIRS)
    ang_r = row[:, None] * inv
    ang_c = col[:, None] * inv
    cos = jnp.concatenate([jnp.cos(ang_r)] * 2 + [jnp.cos(ang_c)] * 2, axis=1)
    sin = jnp.concatenate([-jnp.sin(ang_r), jnp.sin(ang_r),
                           -jnp.sin(ang_c), jnp.sin(ang_c)], axis=1)
    return jnp.concatenate([cos, cos], axis=1), jnp.concatenate([sin, sin], axis=1)


def _softmax_pv(scores, values, sink):
    m = jnp.maximum(functools.reduce(
        jnp.maximum, [jnp.max(s, axis=1, keepdims=True) for s in scores]), sink)
    den = jnp.exp(sink - m)
    acc = None
    for s, v in zip(scores, values):
        p = jnp.exp(s - m)
        den = den + jnp.sum(p, axis=1, keepdims=True)
        pv = _dot(p.astype(BF16), v)
        acc = pv if acc is None else acc + pv
    return acc / den


def _attn_ctx_body(sink_ref, q_ref, k_ref, v_ref, x_ref, gt_ref, wo_ref, o_ref, head_out):
    n_heads = q_ref.shape[1] // HEAD_DIM
    for kv in range(n_heads // GQA_GROUP):
        ks = slice(kv * HEAD_DIM, (kv + 1) * HEAD_DIM)
        kb = k_ref[:, ks].astype(BF16)
        vb = v_ref[:, ks].astype(BF16)
        for grp in range(GQA_GROUP):
            hd = kv * GQA_GROUP + grp
            hs = slice(hd * HEAD_DIM, (hd + 1) * HEAD_DIM)
            s = _dot_nt(q_ref[:, hs], kb)
            head_out[:, hs] = _softmax_pv([s], [vb], sink_ref[hd]).astype(BF16)
    y = _dot(head_out[...], wo_ref[...])
    o_ref[...] = x_ref[...] + gt_ref[...] * y


def _attn_ctx(q, k, v, sink, x, mod, w_o):
    b, s, d = x.shape
    kv_dim = k.shape[2]
    tok = lambda w: pl.BlockSpec((None, s, w), lambda i: (i, 0, 0))
    return pl.pallas_call(
        _attn_ctx_body,
        out_shape=jax.ShapeDtypeStruct(x.shape, F32),
        grid=(b,),
        in_specs=[pl.BlockSpec(memory_space=pltpu.SMEM),
                  tok(d), tok(kv_dim), tok(kv_dim), tok(d),
                  _mod_block(d, 5, False), _const_spec(w_o.shape)],
        out_specs=tok(d),
        scratch_shapes=[pltpu.VMEM((s, d), BF16)],
        compiler_params=_params("parallel"),
        name="attn_ctx",
    )(sink, q, k, v, x, mod, w_o)


def _attn_lat_body(sink_ref, q_ref, k_ref, v_ref, kc_ref, vc_ref, x_ref, gt_ref, wo_ref,
                   o_ref, head_out):
    seq = k_ref.shape[0]
    span = 3 * ATTN_BLK
    j = pl.program_id(1)
    start = pl.multiple_of(jnp.clip((j - 1) * ATTN_BLK, 0, seq - span), ATTN_BLK)
    qpos = j * ATTN_BLK + lax.broadcasted_iota(jnp.int32, (ATTN_BLK, span), 0)
    kpos = start + lax.broadcasted_iota(jnp.int32, (ATTN_BLK, span), 1)
    valid = jnp.abs(qpos - kpos) <= ATTN_BLK
    n_heads = q_ref.shape[1] // HEAD_DIM
    for kv in range(n_heads // GQA_GROUP):
        ks = slice(kv * HEAD_DIM, (kv + 1) * HEAD_DIM)
        kl = k_ref[pl.ds(start, span), ks].astype(BF16)
        vl = v_ref[pl.ds(start, span), ks].astype(BF16)
        kc = kc_ref[:, ks].astype(BF16)
        vc = vc_ref[:, ks].astype(BF16)
        for grp in range(GQA_GROUP):
            hd = kv * GQA_GROUP + grp
            hs = slice(hd * HEAD_DIM, (hd + 1) * HEAD_DIM)
            qh = q_ref[:, hs]
            s_loc = jnp.where(valid, _dot_nt(qh, kl), NEG_INF)
            s_ctx = _dot_nt(qh, kc)
            head_out[:, hs] = _softmax_pv([s_loc, s_ctx], [vl, vc], sink_ref[hd]).astype(BF16)
    y = _dot(head_out[...], wo_ref[...])
    o_ref[...] = x_ref[...] + gt_ref[...] * y


def _attn_lat(q, k, v, kc, vc, sink, x, mod, w_o):
    b, l, d = x.shape
    kv_dim = k.shape[2]
    past = kc.shape[1]
    assert l % ATTN_BLK == 0 and l >= 3 * ATTN_BLK
    blk = lambda w: pl.BlockSpec((None, ATTN_BLK, w), lambda i, j: (i, j, 0))
    full = lambda n: pl.BlockSpec((None, n, kv_dim), lambda i, j: (i, 0, 0))
    return pl.pallas_call(
        _attn_lat_body,
        out_shape=jax.ShapeDtypeStruct(x.shape, F32),
        grid=(b, l // ATTN_BLK),
        in_specs=[pl.BlockSpec(memory_space=pltpu.SMEM),
                  blk(d), full(l), full(l), full(past), full(past), blk(d),
                  _mod_block(d, 5, True), _const_spec(w_o.shape)],
        out_specs=blk(d),
        scratch_shapes=[pltpu.VMEM((ATTN_BLK, d), BF16)],
        compiler_params=_params("parallel", "arbitrary"),
        name="attn_lat",
    )(sink, q, k, v, kc, vc, x, mod, w_o)


def _segment_apply(z, mat_bf16):
    return jnp.concatenate(
        [_dot2_exact_rhs(z[:, p * LANES:(p + 1) * LANES], mat_bf16)
         for p in range(z.shape[1] // LANES)], axis=1)


def _rwkv_pre_body(x_ref, xp_ref, xn_ref, sh_ref, sc_ref, g_ref, mu_ref, wrkv_ref,
                   w0_ref, w1_ref, w2_ref, a0_ref, a1_ref, a2_ref, g1_ref, g2_ref,
                   kk_scale_ref, ka_ref,
                   r_ref, v_ref, kk_ref, lw_ref, kd_ref, a_ref, gg_ref):
    t = pl.program_id(1)
    nt = pl.num_programs(1)
    g = g_ref[...]
    sh = sh_ref[...]
    sc = sc_ref[...]
    h = _mod_norm(x_ref[...], g, sh, sc)
    tm = h.shape[0]
    halo = xp_ref.shape[0]
    h_before = _mod_norm(xp_ref[...], g, sh, sc)[halo - 1:halo]
    h_after = _mod_norm(xn_ref[...], g, sh, sc)[0:1]
    h_before = jnp.where(t == 0, 0.0, h_before)
    h_after = jnp.where(t == nt - 1, 0.0, h_after)
    row = lax.broadcasted_iota(jnp.int32, h.shape, 0)
    prev = jnp.where(row == 0, h_before, pltpu.roll(h, 1, 0))
    nxt = jnp.where(row == tm - 1, h_after, pltpu.roll(h, tm - 1, 0))
    xx = 0.5 * (prev + nxt) - h
    mix = lambda i: (h + xx * mu_ref[i:i + 1, :]).astype(BF16)
    r = _dot(mix(0), wrkv_ref[0])
    xw = mix(1)
    k = _dot(mix(2), wrkv_ref[1])
    v = _dot(mix(3), wrkv_ref[2])
    xa = mix(4)
    xg = mix(5)
    r_ref[...] = r
    v_ref[...] = v
    kk = k * kk_scale_ref[...]
    ssq = _segment_apply(kk * kk, _head_ones(1.0))
    kk_ref[...] = kk * lax.rsqrt(ssq + KK_EPS)
    ka = ka_ref[...]
    for d in range(2):
        lora = _dot(jnp.tanh(_dot(xw, w1_ref[d])).astype(BF16), w2_ref[d])
        z = -(w0_ref[d:d + 1, :] + lora)
        softplus = jnp.maximum(z, 0.0) + jnp.log(1.0 + jnp.exp(-jnp.abs(z)))
        lw_ref[d] = -jnp.exp(-softplus - 0.5)
        a = jax.nn.sigmoid(a0_ref[d:d + 1, :] + _dot(_dot(xa, a1_ref[d]).astype(BF16), a2_ref[d]))
        a_ref[d] = a
        kd_ref[d] = k * (1.0 + (a - 1.0) * ka)
        gg_ref[d] = _dot(jax.nn.sigmoid(_dot(xg, g1_ref[d])).astype(BF16), g2_ref[d])


def _rwkv_pre(x, mod, per_batch, g, p, tm):
    b, l, d = x.shape
    halo = 8
    nh = l // halo
    tok = pl.BlockSpec((None, tm, d), lambda i, t: (i, t, 0))
    tok2 = pl.BlockSpec((2, None, tm, d), lambda i, t: (0, i, t, 0))
    before = pl.BlockSpec((None, halo, d),
                          lambda i, t: (i, jnp.maximum(t * (tm // halo) - 1, 0), 0))
    after = pl.BlockSpec((None, halo, d),
                         lambda i, t: (i, jnp.minimum((t + 1) * (tm // halo), nh - 1), 0))
    one = jax.ShapeDtypeStruct((b, l, d), F32)
    two = jax.ShapeDtypeStruct((2, b, l, d), F32)
    consts = [g.reshape(1, d), p["mu"], p["w_rkv"], p["w0"], p["w1"], p["w2"], p["a0"],
              p["a1"], p["a2"], p["g1"], p["g2"], p["k_k"].reshape(1, d),
              p["k_a"].reshape(1, d)]
    return pl.pallas_call(
        _rwkv_pre_body,
        out_shape=(one, one, one, two, two, two, two),
        grid=(b, l // tm),
        in_specs=[tok, before, after, _mod_block(d, 3, per_batch), _mod_block(d, 4, per_batch)]
                 + [_const_spec(c.shape) for c in consts],
        out_specs=(tok, tok, tok, tok2, tok2, tok2, tok2),
        compiler_params=_params("parallel", "parallel"),
        name="rwkv_pre",
    )(x, x, x, mod, mod, *consts)


def _stack_heads(x, first_head):
    return jnp.concatenate([jnp.where(first_head, x, 0.0), jnp.where(first_head, 0.0, x)],
                           axis=0)


def _wkv_body(r_ref, lw_ref, k_ref, v_ref, kk_ref, ag_ref, s0_ref, lng_ref, lnb_ref, rk_ref,
              z_ref, sout_ref, state, *, reverse):
    c = pl.program_id(2)
    nc = pl.num_programs(2)
    chunk = r_ref.shape[0]
    rows = 2 * chunk

    @pl.when(c == 0)
    def _():
        state[...] = s0_ref[...]

    r = r_ref[...]
    lw = lw_ref[...]
    k = k_ref[...]
    v = v_ref[...]
    kk = kk_ref[...]
    b_vec = kk * ag_ref[...]

    ti = lax.broadcasted_iota(jnp.int32, (chunk, chunk), 0)
    si = lax.broadcasted_iota(jnp.int32, (chunk, chunk), 1)
    tri = jnp.where((si >= ti) if reverse else (si <= ti), 1.0, 0.0).astype(BF16)
    lw_hi = lw.astype(BF16)
    lw_mid = (lw - lw_hi.astype(F32)).astype(BF16)
    lw_lo = (lw - lw_hi.astype(F32) - lw_mid.astype(F32)).astype(BF16)
    cum = _dot(tri, lw_hi) + (_dot(tri, lw_mid) + _dot(tri, lw_lo))
    total = cum[0:1] if reverse else cum[chunk - 1:chunk]
    dec_in = jnp.exp(cum)
    dec_out = jnp.exp(-cum)
    dec_end = jnp.exp(total - cum)

    lane = lax.broadcasted_iota(jnp.int32, (chunk, LANES), 1)
    first_head = lane < HEAD_DIM
    lhs = jnp.concatenate([_stack_heads(-kk * jnp.exp(cum - lw), first_head),
                           _stack_heads(r * dec_in, first_head)], axis=0)
    rhs = jnp.concatenate([_stack_heads(b_vec * dec_out, first_head),
                           _stack_heads(k * dec_out, first_head)], axis=0)
    vb = _stack_heads(v, first_head)

    gram = _dot3(lhs, rhs, _dot_nt)
    tt = lax.broadcasted_iota(jnp.int32, (rows, rows), 0) % chunk
    ss = lax.broadcasted_iota(jnp.int32, (rows, rows), 1) % chunk
    strict = (ss > tt) if reverse else (ss < tt)
    incl = (ss >= tt) if reverse else (ss <= tt)
    l_ab = jnp.where(strict, gram[:rows, :rows], 0.0)
    l_ak = jnp.where(strict, gram[:rows, rows:], 0.0)
    m_rb = jnp.where(incl, gram[rows:, :rows], 0.0)
    m_rk = jnp.where(incl, gram[rows:, rows:], 0.0)

    s_prev = state[...]
    from_state = _dot3(lhs, s_prev, _dot_nt)
    from_v = _dot3(jnp.concatenate([l_ak, m_rk], axis=0), vb)
    x_rhs = from_state[:rows] + from_v[:rows]

    eye = jnp.where(lax.broadcasted_iota(jnp.int32, (rows, rows), 0)
                    == lax.broadcasted_iota(jnp.int32, (rows, rows), 1), 1.0, 0.0)
    inv = eye + l_ab
    power = _dot3(l_ab, l_ab)
    span = 2
    while 2 * span < chunk:
        both = _dot3(jnp.concatenate([inv, power], axis=0), power)
        inv = inv + both[:rows]
        power = both[rows:]
        span *= 2
    inv = inv + _dot3(inv, power)

    u = _dot3(inv, x_rhs)
    y_stacked = from_state[rows:] + from_v[rows:] + _dot3(m_rb, u)
    y = y_stacked[:chunk] + y_stacked[chunk:]

    upd = _dot3(jnp.concatenate([u, vb], axis=0),
                jnp.concatenate([_stack_heads(b_vec * dec_end, first_head),
                                 _stack_heads(k * dec_end, first_head)], axis=0), _dot_tn)
    s_new = s_prev * jnp.exp(total) + upd
    state[...] = s_new

    @pl.when(c == nc - 1)
    def _():
        sout_ref[...] = s_new

    avg = _head_ones(1.0 / HEAD_DIM)
    mean = _dot2_exact_rhs(y, avg)
    cen = y - mean
    var = _dot2_exact_rhs(cen * cen, avg)
    bonus = _dot2_exact_rhs(r * k * rk_ref[...], _head_ones(1.0)) * v
    z_ref[...] = cen * lax.rsqrt(var + GN_EPS) * lng_ref[...] + lnb_ref[...] + bonus


def _wkv_scan(r, lw, kd, v, kk, ag, s0, ln_g, ln_b, r_k, reverse):
    b, l, d = r.shape
    n_pairs = d // LANES
    nc = l // SCAN_CHUNK
    if reverse:
        tok = pl.BlockSpec((None, SCAN_CHUNK, LANES), lambda i, p, c: (i, nc - 1 - c, p))
    else:
        tok = pl.BlockSpec((None, SCAN_CHUNK, LANES), lambda i, p, c: (i, c, p))
    st = pl.BlockSpec((None, None, LANES, LANES), lambda i, p, c: (i, p, 0, 0))
    vec = pl.BlockSpec((1, LANES), lambda i, p, c: (0, p))
    return pl.pallas_call(
        functools.partial(_wkv_body, reverse=reverse),
        out_shape=(jax.ShapeDtypeStruct((b, l, d), F32),
                   jax.ShapeDtypeStruct((b, n_pairs, LANES, LANES), F32)),
        grid=(b, n_pairs, nc),
        in_specs=[tok] * 6 + [st, vec, vec, vec],
        out_specs=(tok, st),
        scratch_shapes=[pltpu.VMEM((LANES, LANES), F32)],
        compiler_params=_params("parallel", "parallel", "arbitrary"),
        name="wkv_rev" if reverse else "wkv_fwd",
    )(r, lw, kd, v, kk, ag, s0, ln_g.reshape(1, d), ln_b.reshape(1, d), r_k.reshape(1, d))


def _pair_states(s):
    b, h, n, _ = s.shape
    s = s.reshape(b, h // 2, 2, n, n)
    z = jnp.zeros_like(s[:, :, 0])
    top = jnp.concatenate([s[:, :, 0], z], axis=-1)
    bot = jnp.concatenate([z, s[:, :, 1]], axis=-1)
    return jnp.concatenate([top, bot], axis=-2)


def _unpair_states(s):
    b, p, _, _ = s.shape
    n = HEAD_DIM
    return jnp.stack([s[:, :, :n, :n], s[:, :, n:, n:]], axis=2).reshape(b, 2 * p, n, n)


def _rwkv_post_body(zf_ref, zb_ref, gg_ref, x_ref, gt_ref, wo_ref, o_ref):
    y = (zf_ref[...] * gg_ref[0] + zb_ref[...] * gg_ref[1]).astype(BF16)
    o_ref[...] = x_ref[...] + gt_ref[...] * _dot(y, wo_ref[...])


def _rwkv_post(zf, zb, gg, x, mod, per_batch, w_o, tm):
    b, l, d = x.shape
    tok = pl.BlockSpec((None, tm, d), lambda i, t: (i, t, 0))
    tok2 = pl.BlockSpec((2, None, tm, d), lambda i, t: (0, i, t, 0))
    return pl.pallas_call(
        _rwkv_post_body,
        out_shape=jax.ShapeDtypeStruct(x.shape, F32),
        grid=(b, l // tm),
        in_specs=[tok, tok, tok2, tok, _mod_block(d, 5, per_batch), _const_spec(w_o.shape)],
        out_specs=tok,
        compiler_params=_params("parallel", "parallel"),
        name="rwkv_post",
    )(zf, zb, gg, x, mod, w_o)


def _final_norm_body(x_ref, g_ref, o_ref):
    x = x_ref[...]
    ms = jnp.mean(x * x, axis=-1, keepdims=True)
    o_ref[...] = x * lax.rsqrt(ms + RMS_EPS) * g_ref[...]


def _final_norm(x, g, tm):
    b, l, d = x.shape
    tok = pl.BlockSpec((None, tm, d), lambda i, t: (i, t, 0))
    return pl.pallas_call(
        _final_norm_body,
        out_shape=jax.ShapeDtypeStruct(x.shape, F32),
        grid=(b, l // tm),
        in_specs=[tok, _const_spec((1, d))],
        out_specs=tok,
        compiler_params=_params("parallel", "parallel"),
        name="final_norm",
    )(x, g.reshape(1, d))


def _token_tile(l):
    return 256 if l % 256 == 0 else 128


def kernel(x_prompt, x_sample, cache_k, cache_v, state_wkv, c, c_ctx, ada_w, ada_b, norm_g, ffn_w_in, ffn_w_out, attn_w_qkv, attn_w_o, attn_sink, rwkv_mu, rwkv_w_rkv, rwkv_w0, rwkv_w1, rwkv_w2, rwkv_a0, rwkv_a1, rwkv_a2, rwkv_g1, rwkv_g2, rwkv_k_k, rwkv_k_a, rwkv_r_k, rwkv_ln_g, rwkv_ln_b, rwkv_w_o, norm_f):
    depth = ada_w.shape[0]
    bp, sp, d = x_prompt.shape
    bs, ls, _ = x_sample.shape
    n_heads = d // HEAD_DIM
    kv_dim = (n_heads // GQA_GROUP) * HEAD_DIM
    tm_p = _token_tile(sp)
    tm_s = _token_tile(ls)

    rows = -(-(1 + bs) // 8) * 8
    cond = jnp.zeros((rows, d), F32).at[0].set(c_ctx).at[1:1 + bs].set(c)
    mods = _ada(cond, ada_w, ada_b)

    ffn_w_in_b = ffn_w_in.astype(BF16)
    ffn_w_out_b = ffn_w_out.astype(BF16)
    rope_tables = _rope_tables(ls, GRID_W)

    xp, xs = x_prompt, x_sample
    ctx_k, ctx_v, ctx_state = [], [], []
    for l in range(depth):
        mod_p = mods[l, 0:1].reshape(1, N_ADA, 1, d)
        mod_s = mods[l, 1:1 + bs].reshape(bs, N_ADA, 1, d)
        xp = _ffn(xp, mod_p, (0, 1, 2), False, norm_g[l, 0], ffn_w_in_b[l, 0], ffn_w_out_b[l, 0], tm_p)
        xs = _ffn(xs, mod_s, (0, 1, 2), True, norm_g[l, 0], ffn_w_in_b[l, 0], ffn_w_out_b[l, 0], tm_s)
        i = l // 2
        if l % 2 == 0:
            w_qkv = attn_w_qkv[i].astype(BF16)
            w_o = attn_w_o[i].astype(BF16)
            q, k, v = _qkv(xp, mod_p, False, norm_g[l, 1], w_qkv, None, tm_p)
            ctx_k.append(k.reshape(bp, sp, kv_dim // HEAD_DIM, HEAD_DIM))
            ctx_v.append(v.reshape(bp, sp, kv_dim // HEAD_DIM, HEAD_DIM))
            xp = _attn_ctx(q, k, v, attn_sink[i], xp, mod_p, w_o)
            q, k, v = _qkv(xs, mod_s, True, norm_g[l, 1], w_qkv, rope_tables, tm_s)
            past = cache_k.shape[2]
            xs = _attn_lat(q, k, v, cache_k[:, i].reshape(bs, past, kv_dim),
                           cache_v[:, i].reshape(bs, past, kv_dim), attn_sink[i], xs, mod_s, w_o)
        else:
            p = dict(mu=rwkv_mu[i], w_rkv=rwkv_w_rkv[i].astype(BF16), w0=rwkv_w0[i],
                     w1=rwkv_w1[i].astype(BF16), w2=rwkv_w2[i].astype(BF16), a0=rwkv_a0[i],
                     a1=rwkv_a1[i].astype(BF16), a2=rwkv_a2[i].astype(BF16),
                     g1=rwkv_g1[i].astype(BF16), g2=rwkv_g2[i].astype(BF16),
                     k_k=rwkv_k_k[i], k_a=rwkv_k_a[i])
            w_o = rwkv_w_o[i].astype(BF16)
            r_k = rwkv_r_k[i].reshape(2, d)
            zero_state = jnp.zeros((bp, n_heads // 2, LANES, LANES), F32)
            new_states = []
            for x, mod, per_batch, tm, is_prompt in ((xp, mod_p, False, tm_p, True),
                                                     (xs, mod_s, True, tm_s, False)):
                r, v, kk, lw, kd, ag, gg = _rwkv_pre(x, mod, per_batch, norm_g[l, 1], p, tm)
                zs = []
                for dr in range(2):
                    s0 = zero_state if is_prompt else _pair_states(state_wkv[:, i, dr])
                    z, s_end = _wkv_scan(r, lw[dr], kd[dr], v, kk, ag[dr], s0,
                                         rwkv_ln_g[i, dr], rwkv_ln_b[i, dr], r_k[dr], dr == 1)
                    zs.append(z)
                    if is_prompt:
                        new_states.append(_unpair_states(s_end))
                x = _rwkv_post(zs[0], zs[1], gg, x, mod, per_batch, w_o, tm)
                if is_prompt:
                    xp = x
                else:
                    xs = x
            ctx_state.append(jnp.stack(new_states, axis=1))
        xp = _ffn(xp, mod_p, (6, 7, 8), False, norm_g[l, 2], ffn_w_in_b[l, 1], ffn_w_out_b[l, 1], tm_p)
        xs = _ffn(xs, mod_s, (6, 7, 8), True, norm_g[l, 2], ffn_w_in_b[l, 1], ffn_w_out_b[l, 1], tm_s)

    y_prompt = _final_norm(xp, norm_f, tm_p)
    y_sample = _final_norm(xs, norm_f, tm_s)
    return (y_prompt, y_sample, jnp.stack(ctx_k, axis=1), jnp.stack(ctx_v, axis=1),
            jnp.stack(ctx_state, axis=1))
```

```python
import functools

import jax
import jax.numpy as jnp
from jax import lax
from jax.experimental import pallas as pl
from jax.experimental.pallas import tpu as pltpu

F32 = jnp.float32
BF16 = jnp.bfloat16

HEAD_DIM = 64
GQA_GROUP = 4
LANES = 128
MXU_DEPTH = 256
N_ADA = 9
ATTN_BLK = 128
GRID_W = 64
ROPE_BASE = 10000.0
ROPE_PAIRS = HEAD_DIM // 4
RMS_EPS = 1e-6
GN_EPS = 64e-5
KK_EPS = 1e-12
NEG_INF = -1e30
SCAN_CHUNK = 64
SCAN_PAIRS = 8
VMEM_LIMIT = 56 * 1024 * 1024


def _params(*sem):
    return pltpu.CompilerParams(dimension_semantics=sem, vmem_limit_bytes=VMEM_LIMIT)


def _const_spec(shape):
    nd = len(shape)
    return pl.BlockSpec(shape, lambda *_: (0,) * nd, pipeline_mode=pl.Buffered(1))


def _mod_block(d_model, chunk, per_batch):
    def index(b, *_):
        return (b if per_batch else 0, chunk, 0, 0)
    return pl.BlockSpec((None, None, 1, d_model), index)


def _mod_norm(x, g, shift, scale):
    ms = jnp.mean(x * x, axis=-1, keepdims=True)
    return (x * lax.rsqrt(ms + RMS_EPS) * g) * (1.0 + scale) + shift


def _dot(a, b):
    return jnp.dot(a, b, preferred_element_type=F32)


def _dot_nt(a, b):
    return lax.dot_general(a, b, (((1,), (1,)), ((), ())), preferred_element_type=F32)


def _dot_tn(a, b):
    return lax.dot_general(a, b, (((0,), (0,)), ((), ())), preferred_element_type=F32)


def _split(a):
    hi = a.astype(BF16)
    lo = (a - hi.astype(F32)).astype(BF16)
    return hi, lo


def _dot3(a, b):
    ah, al = _split(a)
    bh, bl = _split(b)
    if 2 * a.shape[1] > MXU_DEPTH:
        return _dot(ah, bh) + (_dot(ah, bl) + _dot(al, bh))
    return (_dot(jnp.concatenate([ah, al], axis=1), jnp.concatenate([bh, bh], axis=0))
            + _dot(ah, bl))


def _dot3_nt(a, b):
    ah, al = _split(a)
    bh, bl = _split(b)
    return (_dot_nt(jnp.concatenate([ah, al], axis=1), jnp.concatenate([bh, bh], axis=1))
            + _dot_nt(ah, bl))


def _dot3_tn(a, b):
    ah, al = _split(a)
    bh, bl = _split(b)
    return _dot_tn(ah, bh) + (_dot_tn(ah, bl) + _dot_tn(al, bh))


def _dot2_exact_rhs(a, b_bf16):
    ah, al = _split(a)
    return _dot(jnp.concatenate([ah, al], axis=1), jnp.concatenate([b_bf16, b_bf16], axis=0))


def _head_ones(scale):
    r = lax.broadcasted_iota(jnp.int32, (LANES, LANES), 0) // HEAD_DIM
    c = lax.broadcasted_iota(jnp.int32, (LANES, LANES), 1) // HEAD_DIM
    return jnp.where(r == c, scale, 0.0).astype(BF16)


def _ada_body(c_ref, w_ref, b_ref, o_ref):
    c = c_ref[...]
    s = (c * jax.nn.sigmoid(c)).astype(BF16)
    o_ref[...] = _dot(s, w_ref[...].astype(BF16)) + b_ref[...]


def _ada(cond, ada_w, ada_b):
    depth, d_model, n_out = ada_w.shape
    rows = cond.shape[0]
    tn = d_model
    return pl.pallas_call(
        _ada_body,
        out_shape=jax.ShapeDtypeStruct((depth, rows, n_out), F32),
        grid=(depth, n_out // tn),
        in_specs=[
            pl.BlockSpec((rows, d_model), lambda l, n: (0, 0)),
            pl.BlockSpec((None, d_model, tn), lambda l, n: (l, 0, n)),
            pl.BlockSpec((None, 1, tn), lambda l, n: (l, 0, n)),
        ],
        out_specs=pl.BlockSpec((None, rows, tn), lambda l, n: (l, 0, n)),
        compiler_params=_params("parallel", "parallel"),
        name="ada",
    )(cond, ada_w, ada_b.reshape(depth, 1, n_out))


def _ffn_body(x_ref, sh_ref, sc_ref, gt_ref, g_ref, win_ref, wout_ref, o_ref):
    x = x_ref[...]
    h = _mod_norm(x, g_ref[...], sh_ref[...], sc_ref[...]).astype(BF16)
    hh = _dot(h, win_ref[...])
    d_ff = hh.shape[1] // 2
    gate = hh[:, :d_ff]
    act = (gate * jax.nn.sigmoid(gate) * hh[:, d_ff:]).astype(BF16)
    y = _dot(act, wout_ref[...])
    o_ref[...] = x + (0.5 * gt_ref[...]) * y


def _ffn(x, mod, chunks, per_batch, g, w_in, w_out, tm):
    b, l, d = x.shape
    tok = pl.BlockSpec((None, tm, d), lambda i, t: (i, t, 0))
    return pl.pallas_call(
        _ffn_body,
        out_shape=jax.ShapeDtypeStruct(x.shape, F32),
        grid=(b, l // tm),
        in_specs=[
            tok,
            _mod_block(d, chunks[0], per_batch),
            _mod_block(d, chunks[1], per_batch),
            _mod_block(d, chunks[2], per_batch),
            _const_spec((1, d)),
            _const_spec(w_in.shape),
            _const_spec(w_out.shape),
        ],
        out_specs=tok,
        compiler_params=_params("parallel", "parallel"),
        name="ffn",
    )(x, mod, mod, mod, g.reshape(1, d), w_in, w_out)


def _rope(x, cos, sin_signed):
    width = x.shape[1]
    lane = lax.broadcasted_iota(jnp.int32, x.shape, 1)
    first_half = (lane % (2 * ROPE_PAIRS)) < ROPE_PAIRS
    partner = jnp.where(first_half,
                        pltpu.roll(x, width - ROPE_PAIRS, 1),
                        pltpu.roll(x, ROPE_PAIRS, 1))
    return x * cos + partner * sin_signed


def _qkv_body(*refs, rope, d_model, kv_dim):
    if rope:
        x_ref, sh_ref, sc_ref, g_ref, w_ref, cos_ref, sin_ref, q_ref, k_ref, v_ref = refs
    else:
        x_ref, sh_ref, sc_ref, g_ref, w_ref, q_ref, k_ref, v_ref = refs
    h = _mod_norm(x_ref[...], g_ref[...], sh_ref[...], sc_ref[...]).astype(BF16)
    qkv = _dot(h, w_ref[...])
    q = qkv[:, :d_model]
    k = qkv[:, d_model:d_model + kv_dim]
    v = qkv[:, d_model + kv_dim:]
    if rope:
        cos = cos_ref[...]
        sin = sin_ref[...]
        q = _rope(q, jnp.concatenate([cos] * (d_model // LANES), axis=1),
                  jnp.concatenate([sin] * (d_model // LANES), axis=1))
        k = _rope(k, jnp.concatenate([cos] * (kv_dim // LANES), axis=1),
                  jnp.concatenate([sin] * (kv_dim // LANES), axis=1))
    q_ref[...] = (q * (HEAD_DIM ** -0.5)).astype(BF16)
    k_ref[...] = k
    v_ref[...] = v


def _qkv(x, mod, per_batch, g, w_qkv, rope_tables, tm):
    b, l, d = x.shape
    kv_dim = (w_qkv.shape[1] - d) // 2
    rope = rope_tables is not None
    tok = lambda w: pl.BlockSpec((None, tm, w), lambda i, t: (i, t, 0))
    in_specs = [tok(d), _mod_block(d, 3, per_batch), _mod_block(d, 4, per_batch),
                _const_spec((1, d)), _const_spec(w_qkv.shape)]
    args = [x, mod, mod, g.reshape(1, d), w_qkv]
    if rope:
        tab = pl.BlockSpec((tm, LANES), lambda i, t: (t, 0))
        in_specs += [tab, tab]
        args += list(rope_tables)
    return pl.pallas_call(
        functools.partial(_qkv_body, rope=rope, d_model=d, kv_dim=kv_dim),
        out_shape=(jax.ShapeDtypeStruct((b, l, d), BF16),
                   jax.ShapeDtypeStruct((b, l, kv_dim), F32),
                   jax.ShapeDtypeStruct((b, l, kv_dim), F32)),
        grid=(b, l // tm),
        in_specs=in_specs,
        out_specs=(tok(d), tok(kv_dim), tok(kv_dim)),
        compiler_params=_params("parallel", "parallel"),
        name="qkv_rope" if rope else "qkv",
    )(*args)


def _rope_tables(l, grid_w):
    pos = jnp.arange(l, dtype=jnp.int32)
    row = (pos // grid_w).astype(F32)
    col = (pos % grid_w).astype(F32)
    inv = jnp.power(ROPE_BASE, -jnp.arange(ROPE_PAIRS, dtype=F32) / ROPE_PAIRS)
    ang_r = row[:, None] * inv
    ang_c = col[:, None] * inv
    cos = jnp.concatenate([jnp.cos(ang_r)] * 2 + [jnp.cos(ang_c)] * 2, axis=1)
    sin = jnp.concatenate([-jnp.sin(ang_r), jnp.sin(ang_r),
                           -jnp.sin(ang_c), jnp.sin(ang_c)], axis=1)
    return jnp.concatenate([cos, cos], axis=1), jnp.concatenate([sin, sin], axis=1)


def _softmax_pv(scores, values, sink):
    m = jnp.maximum(functools.reduce(
        jnp.maximum, [jnp.max(s, axis=1, keepdims=True) for s in scores]), sink)
    den = jnp.exp(sink - m)
    acc = None
    for s, v in zip(scores, values):
        p = jnp.exp(s - m)
        den = den + jnp.sum(p, axis=1, keepdims=True)
        pv = _dot(p.astype(BF16), v)
        acc = pv if acc is None else acc + pv
    return acc / den


def _attn_ctx_body(sink_ref, q_ref, k_ref, v_ref, x_ref, gt_ref, wo_ref, o_ref, head_out):
    n_heads = q_ref.shape[1] // HEAD_DIM
    for kv in range(n_heads // GQA_GROUP):
        ks = slice(kv * HEAD_DIM, (kv + 1) * HEAD_DIM)
        kb = k_ref[:, ks].astype(BF16)
        vb = v_ref[:, ks].astype(BF16)
        for grp in range(GQA_GROUP):
            hd = kv * GQA_GROUP + grp
            hs = slice(hd * HEAD_DIM, (hd + 1) * HEAD_DIM)
            s = _dot_nt(q_ref[:, hs], kb)
            head_out[:, hs] = _softmax_pv([s], [vb], sink_ref[hd]).astype(BF16)
    y = _dot(head_out[...], wo_ref[...])
    o_ref[...] = x_ref[...] + gt_ref[...] * y


def _attn_ctx(q, k, v, sink, x, mod, w_o):
    b, s, d = x.shape
    kv_dim = k.shape[2]
    tok = lambda w: pl.BlockSpec((None, s, w), lambda i: (i, 0, 0))
    return pl.pallas_call(
        _attn_ctx_body,
        out_shape=jax.ShapeDtypeStruct(x.shape, F32),
        grid=(b,),
        in_specs=[pl.BlockSpec(memory_space=pltpu.SMEM),
                  tok(d), tok(kv_dim), tok(kv_dim), tok(d),
                  _mod_block(d, 5, False), _const_spec(w_o.shape)],
        out_specs=tok(d),
        scratch_shapes=[pltpu.VMEM((s, d), BF16)],
        compiler_params=_params("parallel"),
        name="attn_ctx",
    )(sink, q, k, v, x, mod, w_o)


def _attn_lat_body(sink_ref, q_ref, k_ref, v_ref, kc_ref, vc_ref, x_ref, gt_ref, wo_ref,
                   o_ref, head_out):
    seq = k_ref.shape[0]
    span = 3 * ATTN_BLK
    j = pl.program_id(1)
    start = pl.multiple_of(jnp.clip((j - 1) * ATTN_BLK, 0, seq - span), ATTN_BLK)
    qpos = j * ATTN_BLK + lax.broadcasted_iota(jnp.int32, (ATTN_BLK, span), 0)
    kpos = start + lax.broadcasted_iota(jnp.int32, (ATTN_BLK, span), 1)
    valid = jnp.abs(qpos - kpos) <= ATTN_BLK
    n_heads = q_ref.shape[1] // HEAD_DIM
    for kv in range(n_heads // GQA_GROUP):
        ks = slice(kv * HEAD_DIM, (kv + 1) * HEAD_DIM)
        kl = k_ref[pl.ds(start, span), ks].astype(BF16)
        vl = v_ref[pl.ds(start, span), ks].astype(BF16)
        kc = kc_ref[:, ks].astype(BF16)
        vc = vc_ref[:, ks].astype(BF16)
        for grp in range(GQA_GROUP):
            hd = kv * GQA_GROUP + grp
            hs = slice(hd * HEAD_DIM, (hd + 1) * HEAD_DIM)
            qh = q_ref[:, hs]
            s_loc = jnp.where(valid, _dot_nt(qh, kl), NEG_INF)
            s_ctx = _dot_nt(qh, kc)
            head_out[:, hs] = _softmax_pv([s_loc, s_ctx], [vl, vc], sink_ref[hd]).astype(BF16)
    y = _dot(head_out[...], wo_ref[...])
    o_ref[...] = x_ref[...] + gt_ref[...] * y


def _attn_lat(q, k, v, kc, vc, sink, x, mod, w_o):
    b, l, d = x.shape
    kv_dim = k.shape[2]
    past = kc.shape[1]
    assert l % ATTN_BLK == 0 and l >= 3 * ATTN_BLK
    blk = lambda w: pl.BlockSpec((None, ATTN_BLK, w), lambda i, j: (i, j, 0))
    full = lambda n: pl.BlockSpec((None, n, kv_dim), lambda i, j: (i, 0, 0))
    return pl.pallas_call(
        _attn_lat_body,
        out_shape=jax.ShapeDtypeStruct(x.shape, F32),
        grid=(b, l // ATTN_BLK),
        in_specs=[pl.BlockSpec(memory_space=pltpu.SMEM),
                  blk(d), full(l), full(l), full(past), full(past), blk(d),
                  _mod_block(d, 5, True), _const_spec(w_o.shape)],
        out_specs=blk(d),
        scratch_shapes=[pltpu.VMEM((ATTN_BLK, d), BF16)],
        compiler_params=_params("parallel", "arbitrary"),
        name="attn_lat",
    )(sink, q, k, v, kc, vc, x, mod, w_o)


def _segment_apply(z, mat_bf16):
    return jnp.concatenate(
        [_dot2_exact_rhs(z[:, p * LANES:(p + 1) * LANES], mat_bf16)
         for p in range(z.shape[1] // LANES)], axis=1)


def _rwkv_pre_body(x_ref, xp_ref, xn_ref, sh_ref, sc_ref, g_ref, mu_ref, wrkv_ref,
                   w0_ref, w1_ref, w2_ref, a0_ref, a1_ref, a2_ref, g1_ref, g2_ref,
                   kk_scale_ref, ka_ref,
                   r_ref, v_ref, kk_ref, lw_ref, kd_ref, a_ref, gg_ref):
    t = pl.program_id(1)
    nt = pl.num_programs(1)
    g = g_ref[...]
    sh = sh_ref[...]
    sc = sc_ref[...]
    h = _mod_norm(x_ref[...], g, sh, sc)
    tm = h.shape[0]
    halo = xp_ref.shape[0]
    h_before = _mod_norm(xp_ref[...], g, sh, sc)[halo - 1:halo]
    h_after = _mod_norm(xn_ref[...], g, sh, sc)[0:1]
    h_before = jnp.where(t == 0, 0.0, h_before)
    h_after = jnp.where(t == nt - 1, 0.0, h_after)
    row = lax.broadcasted_iota(jnp.int32, h.shape, 0)
    prev = jnp.where(row == 0, h_before, pltpu.roll(h, 1, 0))
    nxt = jnp.where(row == tm - 1, h_after, pltpu.roll(h, tm - 1, 0))
    xx = 0.5 * (prev + nxt) - h
    mix = lambda i: (h + xx * mu_ref[i:i + 1, :]).astype(BF16)
    r = _dot(mix(0), wrkv_ref[0])
    xw = mix(1)
    k = _dot(mix(2), wrkv_ref[1])
    v = _dot(mix(3), wrkv_ref[2])
    xa = mix(4)
    xg = mix(5)
    r_ref[...] = r
    v_ref[...] = v
    kk = k * kk_scale_ref[...]
    ssq = _segment_apply(kk * kk, _head_ones(1.0))
    kk_ref[...] = kk * lax.rsqrt(ssq + KK_EPS)
    ka = ka_ref[...]
    for d in range(2):
        lora = _dot(jnp.tanh(_dot(xw, w1_ref[d])).astype(BF16), w2_ref[d])
        z = -(w0_ref[d:d + 1, :] + lora)
        softplus = jnp.maximum(z, 0.0) + jnp.log(1.0 + jnp.exp(-jnp.abs(z)))
        lw_ref[d] = -jnp.exp(-softplus - 0.5)
        a = jax.nn.sigmoid(a0_ref[d:d + 1, :] + _dot(_dot(xa, a1_ref[d]).astype(BF16), a2_ref[d]))
        a_ref[d] = a
        kd_ref[d] = k * (1.0 + (a - 1.0) * ka)
        gg_ref[d] = _dot(jax.nn.sigmoid(_dot(xg, g1_ref[d])).astype(BF16), g2_ref[d])


def _rwkv_pre(x, mod, per_batch, g, p, tm):
    b, l, d = x.shape
    halo = 8
    nh = l // halo
    tok = pl.BlockSpec((None, tm, d), lambda i, t: (i, t, 0))
    tok2 = pl.BlockSpec((2, None, tm, d), lambda i, t: (0, i, t, 0))
    before = pl.BlockSpec((None, halo, d),
                          lambda i, t: (i, jnp.maximum(t * (tm // halo) - 1, 0), 0))
    after = pl.BlockSpec((None, halo, d),
                         lambda i, t: (i, jnp.minimum((t + 1) * (tm // halo), nh - 1), 0))
    one = jax.ShapeDtypeStruct((b, l, d), F32)
    two = jax.ShapeDtypeStruct((2, b, l, d), F32)
    consts = [g.reshape(1, d), p["mu"], p["w_rkv"], p["w0"], p["w1"], p["w2"], p["a0"],
              p["a1"], p["a2"], p["g1"], p["g2"], p["k_k"].reshape(1, d),
              p["k_a"].reshape(1, d)]
    return pl.pallas_call(
        _rwkv_pre_body,
        out_shape=(one, one, one, two, two, two, two),
        grid=(b, l // tm),
        in_specs=[tok, before, after, _mod_block(d, 3, per_batch), _mod_block(d, 4, per_batch)]
                 + [_const_spec(c.shape) for c in consts],
        out_specs=(tok, tok, tok, tok2, tok2, tok2, tok2),
        compiler_params=_params("parallel", "parallel"),
        name="rwkv_pre",
    )(x, x, x, mod, mod, *consts)


def _stack_heads(x, first_head):
    return jnp.concatenate([jnp.where(first_head, x, 0.0), jnp.where(first_head, 0.0, x)],
                           axis=0)


def _wkv_chunks(r, lw, k, v, kk, ag, s_prev, ln_g, ln_b, r_k, reverse):
    each = lambda fn, *cols: [fn(*args) for args in zip(*cols)]
    chunk = r[0].shape[0]
    rows = 2 * chunk
    cat = lambda *parts: jnp.concatenate(parts, axis=0)

    ti = lax.broadcasted_iota(jnp.int32, (chunk, chunk), 0)
    si = lax.broadcasted_iota(jnp.int32, (chunk, chunk), 1)
    tri = jnp.where((si >= ti) if reverse else (si <= ti), 1.0, 0.0).astype(BF16)

    def cumulative(x):
        hi = x.astype(BF16)
        mid = (x - hi.astype(F32)).astype(BF16)
        lo = (x - hi.astype(F32) - mid.astype(F32)).astype(BF16)
        return _dot(jnp.concatenate([tri, tri, tri], axis=1), cat(hi, mid, lo))

    cum = each(cumulative, lw)
    total = [x[0:1] if reverse else x[chunk - 1:chunk] for x in cum]
    b_vec = each(lambda a, b: a * b, kk, ag)

    first_head = lax.broadcasted_iota(jnp.int32, (chunk, LANES), 1) < HEAD_DIM
    stack = lambda x: _stack_heads(x, first_head)
    lhs = each(lambda kk_, r_, cum_, lw_: cat(stack(-kk_ * jnp.exp(cum_ - lw_)),
                                              stack(r_ * jnp.exp(cum_))), kk, r, cum, lw)
    rhs = each(lambda b_, k_, cum_: cat(stack(b_ * jnp.exp(-cum_)), stack(k_ * jnp.exp(-cum_))),
               b_vec, k, cum)
    rhs_end = each(lambda b_, k_, cum_, tot_: cat(stack(b_ * jnp.exp(tot_ - cum_)),
                                                  stack(k_ * jnp.exp(tot_ - cum_))),
                   b_vec, k, cum, total)
    vb = each(stack, v)

    gram = each(_dot3_nt, lhs, rhs)
    from_state = each(_dot3_nt, lhs, s_prev)
    tt = lax.broadcasted_iota(jnp.int32, (rows, rows), 0) % chunk
    ss = lax.broadcasted_iota(jnp.int32, (rows, rows), 1) % chunk
    strict = (ss > tt) if reverse else (ss < tt)
    incl = (ss >= tt) if reverse else (ss <= tt)
    l_ab = [jnp.where(strict, g[:rows, :rows], 0.0) for g in gram]
    l_ak = [jnp.where(strict, g[:rows, rows:], 0.0) for g in gram]
    m_rb = [jnp.where(incl, g[rows:, :rows], 0.0) for g in gram]
    m_rk = [jnp.where(incl, g[rows:, rows:], 0.0) for g in gram]

    from_v = each(lambda a, b, vb_: _dot3(cat(a, b), vb_), l_ak, m_rk, vb)
    x_rhs = each(lambda a, b: a[:rows] + b[:rows], from_state, from_v)

    eye = jnp.where(lax.broadcasted_iota(jnp.int32, (rows, rows), 0)
                    == lax.broadcasted_iota(jnp.int32, (rows, rows), 1), 1.0, 0.0)
    inv = [eye + m for m in l_ab]
    power = each(_dot3, l_ab, l_ab)
    span = 2
    while 2 * span < chunk:
        both = each(lambda i_, p_: _dot3(p_, jnp.concatenate([i_, p_], axis=1)), inv, power)
        inv = each(lambda i_, b_: i_ + b_[:, :rows], inv, both)
        power = [b_[:, rows:] for b_ in both]
        span *= 2
    inv = each(lambda i_, p_: i_ + _dot3(p_, i_), inv, power)

    u = each(_dot3, inv, x_rhs)
    y_stacked = each(lambda fs, fv, m, u_: fs[rows:] + fv[rows:] + _dot3(m, u_),
                     from_state, from_v, m_rb, u)
    y = [ys[:chunk] + ys[chunk:] for ys in y_stacked]
    s_new = each(lambda s, tot_, u_, vb_, re: s * jnp.exp(tot_) + _dot3_tn(cat(u_, vb_), re),
                 s_prev, total, u, vb, rhs_end)

    avg = _head_ones(1.0 / HEAD_DIM)
    ones = _head_ones(1.0)
    mean = [_dot2_exact_rhs(y_, avg) for y_ in y]
    cen = each(lambda a, b: a - b, y, mean)
    var = [_dot2_exact_rhs(c_ * c_, avg) for c_ in cen]
    bonus = each(lambda r_, k_, rk_, v_: _dot2_exact_rhs(r_ * k_ * rk_, ones) * v_, r, k, r_k, v)
    z = each(lambda c_, var_, g_, b_, bo_: c_ * lax.rsqrt(var_ + GN_EPS) * g_ + b_ + bo_,
             cen, var, ln_g, ln_b, bonus)
    return z, s_new


def _wkv_body(r_ref, lw_ref, k_ref, v_ref, kk_ref, ag_ref, s0_ref, lng_ref, lnb_ref, rk_ref,
              z_ref, sout_ref, state, *, reverse):
    c = pl.program_id(2)
    nc = pl.num_programs(2)
    group = state.shape[0]

    @pl.when(c == 0)
    def _():
        state[...] = s0_ref[...]

    lanes = [slice(i * LANES, (i + 1) * LANES) for i in range(group)]
    pairs = lambda ref: [ref[:, ls] for ls in lanes]
    z, s_new = _wkv_chunks(pairs(r_ref), pairs(lw_ref), pairs(k_ref), pairs(v_ref),
                           pairs(kk_ref), pairs(ag_ref), [state[i] for i in range(group)],
                           pairs(lng_ref), pairs(lnb_ref), pairs(rk_ref), reverse)
    for i in range(group):
        z_ref[:, lanes[i]] = z[i]
        state[i] = s_new[i]

    @pl.when(c == nc - 1)
    def _():
        for i in range(group):
            sout_ref[i] = s_new[i]


def _wkv_scan(r, lw, kd, v, kk, ag, s0, ln_g, ln_b, r_k, direction):
    b, l, d = r.shape
    reverse = direction == 1
    group = min(SCAN_PAIRS, d // LANES)
    width = group * LANES
    nc = l // SCAN_CHUNK
    at = (lambda c: nc - 1 - c) if reverse else (lambda c: c)
    tok = pl.BlockSpec((None, SCAN_CHUNK, width), lambda i, p, c: (i, at(c), p))
    tok2 = pl.BlockSpec((None, None, SCAN_CHUNK, width), lambda i, p, c: (direction, i, at(c), p))
    st = pl.BlockSpec((None, group, LANES, LANES), lambda i, p, c: (i, p, 0, 0))
    vec = pl.BlockSpec((1, width), lambda i, p, c: (0, p))
    return pl.pallas_call(
        functools.partial(_wkv_body, reverse=reverse),
        out_shape=(jax.ShapeDtypeStruct((b, l, d), F32),
                   jax.ShapeDtypeStruct(s0.shape, F32)),
        grid=(b, d // width, nc),
        in_specs=[tok, tok2, tok2, tok, tok, tok2, st, vec, vec, vec],
        out_specs=(tok, st),
        scratch_shapes=[pltpu.VMEM((group, LANES, LANES), F32)],
        compiler_params=_params("parallel", "parallel", "arbitrary"),
        name="wkv_rev" if reverse else "wkv_fwd",
    )(r, lw, kd, v, kk, ag, s0, ln_g.reshape(1, d), ln_b.reshape(1, d), r_k.reshape(1, d))


def _pair_states(s):
    b, h, n, _ = s.shape
    s = s.reshape(b, h // 2, 2, n, n)
    z = jnp.zeros_like(s[:, :, 0])
    top = jnp.concatenate([s[:, :, 0], z], axis=-1)
    bot = jnp.concatenate([z, s[:, :, 1]], axis=-1)
    return jnp.concatenate([top, bot], axis=-2)


def _unpair_states(s):
    b, p, _, _ = s.shape
    n = HEAD_DIM
    return jnp.stack([s[:, :, :n, :n], s[:, :, n:, n:]], axis=2).reshape(b, 2 * p, n, n)


def _rwkv_post_body(zf_ref, zb_ref, gg_ref, x_ref, gt_ref, wo_ref, o_ref):
    y = (zf_ref[...] * gg_ref[0] + zb_ref[...] * gg_ref[1]).astype(BF16)
    o_ref[...] = x_ref[...] + gt_ref[...] * _dot(y, wo_ref[...])


def _rwkv_post(zf, zb, gg, x, mod, per_batch, w_o, tm):
    b, l, d = x.shape
    tok = pl.BlockSpec((None, tm, d), lambda i, t: (i, t, 0))
    tok2 = pl.BlockSpec((2, None, tm, d), lambda i, t: (0, i, t, 0))
    return pl.pallas_call(
        _rwkv_post_body,
        out_shape=jax.ShapeDtypeStruct(x.shape, F32),
        grid=(b, l // tm),
        in_specs=[tok, tok, tok2, tok, _mod_block(d, 5, per_batch), _const_spec(w_o.shape)],
        out_specs=tok,
        compiler_params=_params("parallel", "parallel"),
        name="rwkv_post",
    )(zf, zb, gg, x, mod, w_o)


def _final_norm_body(x_ref, g_ref, o_ref):
    x = x_ref[...]
    ms = jnp.mean(x * x, axis=-1, keepdims=True)
    o_ref[...] = x * lax.rsqrt(ms + RMS_EPS) * g_ref[...]


def _final_norm(x, g, tm):
    b, l, d = x.shape
    tok = pl.BlockSpec((None, tm, d), lambda i, t: (i, t, 0))
    return pl.pallas_call(
        _final_norm_body,
        out_shape=jax.ShapeDtypeStruct(x.shape, F32),
        grid=(b, l // tm),
        in_specs=[tok, _const_spec((1, d))],
        out_specs=tok,
        compiler_params=_params("parallel", "parallel"),
        name="final_norm",
    )(x, g.reshape(1, d))


def _token_tile(l):
    return 256 if l % 256 == 0 else 128


def kernel(x_prompt, x_sample, cache_k, cache_v, state_wkv, c, c_ctx, ada_w, ada_b, norm_g, ffn_w_in, ffn_w_out, attn_w_qkv, attn_w_o, attn_sink, rwkv_mu, rwkv_w_rkv, rwkv_w0, rwkv_w1, rwkv_w2, rwkv_a0, rwkv_a1, rwkv_a2, rwkv_g1, rwkv_g2, rwkv_k_k, rwkv_k_a, rwkv_r_k, rwkv_ln_g, rwkv_ln_b, rwkv_w_o, norm_f):
    depth = ada_w.shape[0]
    bp, sp, d = x_prompt.shape
    bs, ls, _ = x_sample.shape
    n_heads = d // HEAD_DIM
    kv_dim = (n_heads // GQA_GROUP) * HEAD_DIM
    tm_p = _token_tile(sp)
    tm_s = _token_tile(ls)

    rows = -(-(1 + bs) // 8) * 8
    cond = jnp.zeros((rows, d), F32).at[0].set(c_ctx).at[1:1 + bs].set(c)
    mods = _ada(cond, ada_w, ada_b)

    ffn_w_in_b = ffn_w_in.astype(BF16)
    ffn_w_out_b = ffn_w_out.astype(BF16)
    rope_tables = _rope_tables(ls, GRID_W)

    xp, xs = x_prompt, x_sample
    ctx_k, ctx_v, ctx_state = [], [], []
    for l in range(depth):
        mod_p = mods[l, 0:1].reshape(1, N_ADA, 1, d)
        mod_s = mods[l, 1:1 + bs].reshape(bs, N_ADA, 1, d)
        xp = _ffn(xp, mod_p, (0, 1, 2), False, norm_g[l, 0], ffn_w_in_b[l, 0], ffn_w_out_b[l, 0], tm_p)
        xs = _ffn(xs, mod_s, (0, 1, 2), True, norm_g[l, 0], ffn_w_in_b[l, 0], ffn_w_out_b[l, 0], tm_s)
        i = l // 2
        if l % 2 == 0:
            w_qkv = attn_w_qkv[i].astype(BF16)
            w_o = attn_w_o[i].astype(BF16)
            q, k, v = _qkv(xp, mod_p, False, norm_g[l, 1], w_qkv, None, tm_p)
            ctx_k.append(k.reshape(bp, sp, kv_dim // HEAD_DIM, HEAD_DIM))
            ctx_v.append(v.reshape(bp, sp, kv_dim // HEAD_DIM, HEAD_DIM))
            xp = _attn_ctx(q, k, v, attn_sink[i], xp, mod_p, w_o)
            q, k, v = _qkv(xs, mod_s, True, norm_g[l, 1], w_qkv, rope_tables, tm_s)
            past = cache_k.shape[2]
            xs = _attn_lat(q, k, v, cache_k[:, i].reshape(bs, past, kv_dim),
                           cache_v[:, i].reshape(bs, past, kv_dim), attn_sink[i], xs, mod_s, w_o)
        else:
            p = dict(mu=rwkv_mu[i], w_rkv=rwkv_w_rkv[i].astype(BF16), w0=rwkv_w0[i],
                     w1=rwkv_w1[i].astype(BF16), w2=rwkv_w2[i].astype(BF16), a0=rwkv_a0[i],
                     a1=rwkv_a1[i].astype(BF16), a2=rwkv_a2[i].astype(BF16),
                     g1=rwkv_g1[i].astype(BF16), g2=rwkv_g2[i].astype(BF16),
                     k_k=rwkv_k_k[i], k_a=rwkv_k_a[i])
            w_o = rwkv_w_o[i].astype(BF16)
            r_k = rwkv_r_k[i].reshape(2, d)
            zero_state = jnp.zeros((bp, n_heads // 2, LANES, LANES), F32)
            new_states = []
            for x, mod, per_batch, tm, is_prompt in ((xp, mod_p, False, tm_p, True),
                                                     (xs, mod_s, True, tm_s, False)):
                r, v, kk, lw, kd, ag, gg = _rwkv_pre(x, mod, per_batch, norm_g[l, 1], p, tm)
                zs = []
                for dr in range(2):
                    s0 = zero_state if is_prompt else _pair_states(state_wkv[:, i, dr])
                    z, s_end = _wkv_scan(r, lw, kd, v, kk, ag, s0,
                                         rwkv_ln_g[i, dr], rwkv_ln_b[i, dr], r_k[dr], dr)
                    zs.append(z)
                    if is_prompt:
                        new_states.append(_unpair_states(s_end))
                x = _rwkv_post(zs[0], zs[1], gg, x, mod, per_batch, w_o, tm)
                if is_prompt:
                    xp = x
                else:
                    xs = x
            ctx_state.append(jnp.stack(new_states, axis=1))
        xp = _ffn(xp, mod_p, (6, 7, 8), False, norm_g[l, 2], ffn_w_in_b[l, 1], ffn_w_out_b[l, 1], tm_p)
        xs = _ffn(xs, mod_s, (6, 7, 8), True, norm_g[l, 2], ffn_w_in_b[l, 1], ffn_w_out_b[l, 1], tm_s)

    y_prompt = _final_norm(xp, norm_f, tm_p)
    y_sample = _final_norm(xs, norm_f, tm_s)
    return (y_prompt, y_sample, jnp.stack(ctx_k, axis=1), jnp.stack(ctx_v, axis=1),
            jnp.stack(ctx_state, axis=1))
```

```python
import functools

import jax
import jax.numpy as jnp
from jax import lax
from jax.experimental import pallas as pl
from jax.experimental.pallas import tpu as pltpu

F32 = jnp.float32
BF16 = jnp.bfloat16

HEAD_DIM = 64
GQA_GROUP = 4
LANES = 128
N_ADA = 9
ATTN_BLK = 128
GRID_W = 64
ROPE_BASE = 10000.0
ROPE_PAIRS = HEAD_DIM // 4
RMS_EPS = 1e-6
GN_EPS = 64e-5
KK_EPS = 1e-12
DECAY_SCALE = 0.6065306597126334
NEG_INF = -1e30
SCAN_CHUNK = 64
SCAN_PAIRS = 8
VMEM_LIMIT = 56 * 1024 * 1024


def _params(*sem):
    return pltpu.CompilerParams(dimension_semantics=sem, vmem_limit_bytes=VMEM_LIMIT)


def _const_spec(shape):
    nd = len(shape)
    return pl.BlockSpec(shape, lambda *_: (0,) * nd, pipeline_mode=pl.Buffered(1))


def _mod_block(d_model, chunk, per_batch):
    def index(b, *_):
        return (b if per_batch else 0, chunk, 0, 0)
    return pl.BlockSpec((None, None, 1, d_model), index)


def _mod_norm(x, g, shift, scale):
    ms = jnp.mean(x * x, axis=-1, keepdims=True)
    return (x * lax.rsqrt(ms + RMS_EPS) * g) * (1.0 + scale) + shift


def _dot(a, b):
    return jnp.dot(a, b, preferred_element_type=F32)


def _dot_nt(a, b):
    return lax.dot_general(a, b, (((1,), (1,)), ((), ())), preferred_element_type=F32)


def _dot_tn(a, b):
    return lax.dot_general(a, b, (((0,), (0,)), ((), ())), preferred_element_type=F32)


def _split(a):
    hi = a.astype(BF16)
    lo = (a - hi.astype(F32)).astype(BF16)
    return hi, lo


def _dot1(a, b):
    return _dot(a.astype(BF16), b.astype(BF16))


def _dot1_nt(a, b):
    return _dot_nt(a.astype(BF16), b.astype(BF16))


def _dot3_tn(a, b):
    ah, al = _split(a)
    bh, bl = _split(b)
    return _dot_tn(ah, bh) + (_dot_tn(ah, bl) + _dot_tn(al, bh))


def _dot2_exact_rhs(a, b_bf16):
    ah, al = _split(a)
    return _dot(jnp.concatenate([ah, al], axis=1), jnp.concatenate([b_bf16, b_bf16], axis=0))


def _head_ones(scale):
    r = lax.broadcasted_iota(jnp.int32, (LANES, LANES), 0) // HEAD_DIM
    c = lax.broadcasted_iota(jnp.int32, (LANES, LANES), 1) // HEAD_DIM
    return jnp.where(r == c, scale, 0.0).astype(BF16)


def _ada_body(c_ref, w_ref, b_ref, o_ref):
    c = c_ref[...]
    s = (c * jax.nn.sigmoid(c)).astype(BF16)
    o_ref[...] = _dot(s, w_ref[...].astype(BF16)) + b_ref[...]


def _ada(cond, ada_w, ada_b):
    depth, d_model, n_out = ada_w.shape
    rows = cond.shape[0]
    tn = d_model
    return pl.pallas_call(
        _ada_body,
        out_shape=jax.ShapeDtypeStruct((depth, rows, n_out), F32),
        grid=(depth, n_out // tn),
        in_specs=[
            pl.BlockSpec((rows, d_model), lambda l, n: (0, 0)),
            pl.BlockSpec((None, d_model, tn), lambda l, n: (l, 0, n)),
            pl.BlockSpec((None, 1, tn), lambda l, n: (l, 0, n)),
        ],
        out_specs=pl.BlockSpec((None, rows, tn), lambda l, n: (l, 0, n)),
        compiler_params=_params("parallel", "parallel"),
        name="ada",
    )(cond, ada_w, ada_b.reshape(depth, 1, n_out))


def _ffn_body(x_ref, sh_ref, sc_ref, gt_ref, g_ref, win_ref, wout_ref, o_ref):
    x = x_ref[...]
    h = _mod_norm(x, g_ref[...], sh_ref[...], sc_ref[...]).astype(BF16)
    hh = _dot(h, win_ref[...])
    d_ff = hh.shape[1] // 2
    gate = hh[:, :d_ff]
    act = (gate * jax.nn.sigmoid(gate) * hh[:, d_ff:]).astype(BF16)
    y = _dot(act, wout_ref[...])
    o_ref[...] = x + (0.5 * gt_ref[...]) * y


def _ffn(x, mod, chunks, per_batch, g, w_in, w_out, tm):
    b, l, d = x.shape
    tok = pl.BlockSpec((None, tm, d), lambda i, t: (i, t, 0))
    return pl.pallas_call(
        _ffn_body,
        out_shape=jax.ShapeDtypeStruct(x.shape, F32),
        grid=(b, l // tm),
        in_specs=[
            tok,
            _mod_block(d, chunks[0], per_batch),
            _mod_block(d, chunks[1], per_batch),
            _mod_block(d, chunks[2], per_batch),
            _const_spec((1, d)),
            _const_spec(w_in.shape),
            _const_spec(w_out.shape),
        ],
        out_specs=tok,
        compiler_params=_params("parallel", "parallel"),
        name="ffn",
    )(x, mod, mod, mod, g.reshape(1, d), w_in, w_out)


def _rope(x, cos, sin_signed):
    width = x.shape[1]
    lane = lax.broadcasted_iota(jnp.int32, x.shape, 1)
    first_half = (lane % (2 * ROPE_PAIRS)) < ROPE_PAIRS
    partner = jnp.where(first_half,
                        pltpu.roll(x, width - ROPE_PAIRS, 1),
                        pltpu.roll(x, ROPE_PAIRS, 1))
    return x * cos + partner * sin_signed


def _qkv_body(*refs, rope, d_model, kv_dim):
    if rope:
        x_ref, sh_ref, sc_ref, g_ref, w_ref, cos_ref, sin_ref, q_ref, k_ref, v_ref = refs
    else:
        x_ref, sh_ref, sc_ref, g_ref, w_ref, q_ref, k_ref, v_ref = refs
    h = _mod_norm(x_ref[...], g_ref[...], sh_ref[...], sc_ref[...]).astype(BF16)
    qkv = _dot(h, w_ref[...])
    q = qkv[:, :d_model]
    k = qkv[:, d_model:d_model + kv_dim]
    v = qkv[:, d_model + kv_dim:]
    if rope:
        cos = cos_ref[...]
        sin = sin_ref[...]
        q = _rope(q, jnp.concatenate([cos] * (d_model // LANES), axis=1),
                  jnp.concatenate([sin] * (d_model // LANES), axis=1))
        k = _rope(k, jnp.concatenate([cos] * (kv_dim // LANES), axis=1),
                  jnp.concatenate([sin] * (kv_dim // LANES), axis=1))
    q_ref[...] = (q * (HEAD_DIM ** -0.5)).astype(BF16)
    k_ref[...] = k
    v_ref[...] = v


def _qkv(x, mod, per_batch, g, w_qkv, rope_tables, tm):
    b, l, d = x.shape
    kv_dim = (w_qkv.shape[1] - d) // 2
    rope = rope_tables is not None
    tok = lambda w: pl.BlockSpec((None, tm, w), lambda i, t: (i, t, 0))
    in_specs = [tok(d), _mod_block(d, 3, per_batch), _mod_block(d, 4, per_batch),
                _const_spec((1, d)), _const_spec(w_qkv.shape)]
    args = [x, mod, mod, g.reshape(1, d), w_qkv]
    if rope:
        tab = pl.BlockSpec((tm, LANES), lambda i, t: (t, 0))
        in_specs += [tab, tab]
        args += list(rope_tables)
    return pl.pallas_call(
        functools.partial(_qkv_body, rope=rope, d_model=d, kv_dim=kv_dim),
        out_shape=(jax.ShapeDtypeStruct((b, l, d), BF16),
                   jax.ShapeDtypeStruct((b, l, kv_dim), F32),
                   jax.ShapeDtypeStruct((b, l, kv_dim), F32)),
        grid=(b, l // tm),
        in_specs=in_specs,
        out_specs=(tok(d), tok(kv_dim), tok(kv_dim)),
        compiler_params=_params("parallel", "parallel"),
        name="qkv_rope" if rope else "qkv",
    )(*args)


def _rope_tables(l, grid_w):
    pos = jnp.arange(l, dtype=jnp.int32)
    row = (pos // grid_w).astype(F32)
    col = (pos % grid_w).astype(F32)
    inv = jnp.power(ROPE_BASE, -jnp.arange(ROPE_PAIRS, dtype=F32) / ROPE_PAIRS)
    ang_r = row[:, None] * inv
    ang_c = col[:, None] * inv
    cos = jnp.concatenate([jnp.cos(ang_r)] * 2 + [jnp.cos(ang_c)] * 2, axis=1)
    sin = jnp.concatenate([-jnp.sin(ang_r), jnp.sin(ang_r),
                           -jnp.sin(ang_c), jnp.sin(ang_c)], axis=1)
    return jnp.concatenate([cos, cos], axis=1), jnp.concatenate([sin, sin], axis=1)


def _softmax_pv(scores, values, sink):
    m = jnp.maximum(functools.reduce(
        jnp.maximum, [jnp.max(s, axis=1, keepdims=True) for s in scores]), sink)
    den = jnp.exp(sink - m)
    acc = None
    for s, v in zip(scores, values):
        p = jnp.exp(s - m)
        den = den + jnp.sum(p, axis=1, keepdims=True)
        pv = _dot(p.astype(BF16), v)
        acc = pv if acc is None else acc + pv
    return acc / den


def _attn_ctx_body(sink_ref, q_ref, k_ref, v_ref, x_ref, gt_ref, wo_ref, o_ref, head_out):
    n_heads = q_ref.shape[1] // HEAD_DIM
    for kv in range(n_heads // GQA_GROUP):
        ks = slice(kv * HEAD_DIM, (kv + 1) * HEAD_DIM)
        kb = k_ref[:, ks].astype(BF16)
        vb = v_ref[:, ks].astype(BF16)
        for grp in range(GQA_GROUP):
            hd = kv * GQA_GROUP + grp
            hs = slice(hd * HEAD_DIM, (hd + 1) * HEAD_DIM)
            s = _dot_nt(q_ref[:, hs], kb)
            head_out[:, hs] = _softmax_pv([s], [vb], sink_ref[hd]).astype(BF16)
    y = _dot(head_out[...], wo_ref[...])
    o_ref[...] = x_ref[...] + gt_ref[...] * y


def _attn_ctx(q, k, v, sink, x, mod, w_o):
    b, s, d = x.shape
    kv_dim = k.shape[2]
    tok = lambda w: pl.BlockSpec((None, s, w), lambda i: (i, 0, 0))
    return pl.pallas_call(
        _attn_ctx_body,
        out_shape=jax.ShapeDtypeStruct(x.shape, F32),
        grid=(b,),
        in_specs=[pl.BlockSpec(memory_space=pltpu.SMEM),
                  tok(d), tok(kv_dim), tok(kv_dim), tok(d),
                  _mod_block(d, 5, False), _const_spec(w_o.shape)],
        out_specs=tok(d),
        scratch_shapes=[pltpu.VMEM((s, d), BF16)],
        compiler_params=_params("parallel"),
        name="attn_ctx",
    )(sink, q, k, v, x, mod, w_o)


def _attn_lat_body(sink_ref, q_ref, k_ref, v_ref, kc_ref, vc_ref, x_ref, gt_ref, wo_ref,
                   o_ref, head_out):
    seq = k_ref.shape[0]
    span = 3 * ATTN_BLK
    j = pl.program_id(1)
    start = pl.multiple_of(jnp.clip((j - 1) * ATTN_BLK, 0, seq - span), ATTN_BLK)
    qpos = j * ATTN_BLK + lax.broadcasted_iota(jnp.int32, (ATTN_BLK, span), 0)
    kpos = start + lax.broadcasted_iota(jnp.int32, (ATTN_BLK, span), 1)
    valid = jnp.abs(qpos - kpos) <= ATTN_BLK
    n_heads = q_ref.shape[1] // HEAD_DIM
    for kv in range(n_heads // GQA_GROUP):
        ks = slice(kv * HEAD_DIM, (kv + 1) * HEAD_DIM)
        kl = k_ref[pl.ds(start, span), ks].astype(BF16)
        vl = v_ref[pl.ds(start, span), ks].astype(BF16)
        kc = kc_ref[:, ks].astype(BF16)
        vc = vc_ref[:, ks].astype(BF16)
        for grp in range(GQA_GROUP):
            hd = kv * GQA_GROUP + grp
            hs = slice(hd * HEAD_DIM, (hd + 1) * HEAD_DIM)
            qh = q_ref[:, hs]
            s_loc = jnp.where(valid, _dot_nt(qh, kl), NEG_INF)
            s_ctx = _dot_nt(qh, kc)
            head_out[:, hs] = _softmax_pv([s_loc, s_ctx], [vl, vc], sink_ref[hd]).astype(BF16)
    y = _dot(head_out[...], wo_ref[...])
    o_ref[...] = x_ref[...] + gt_ref[...] * y


def _attn_lat(q, k, v, kc, vc, sink, x, mod, w_o):
    b, l, d = x.shape
    kv_dim = k.shape[2]
    past = kc.shape[1]
    assert l % ATTN_BLK == 0 and l >= 3 * ATTN_BLK
    blk = lambda w: pl.BlockSpec((None, ATTN_BLK, w), lambda i, j: (i, j, 0))
    full = lambda n: pl.BlockSpec((None, n, kv_dim), lambda i, j: (i, 0, 0))
    return pl.pallas_call(
        _attn_lat_body,
        out_shape=jax.ShapeDtypeStruct(x.shape, F32),
        grid=(b, l // ATTN_BLK),
        in_specs=[pl.BlockSpec(memory_space=pltpu.SMEM),
                  blk(d), full(l), full(l), full(past), full(past), blk(d),
                  _mod_block(d, 5, True), _const_spec(w_o.shape)],
        out_specs=blk(d),
        scratch_shapes=[pltpu.VMEM((ATTN_BLK, d), BF16)],
        compiler_params=_params("parallel", "arbitrary"),
        name="attn_lat",
    )(sink, q, k, v, kc, vc, x, mod, w_o)


def _segment_apply(z, mat_bf16):
    return jnp.concatenate(
        [_dot2_exact_rhs(z[:, p * LANES:(p + 1) * LANES], mat_bf16)
         for p in range(z.shape[1] // LANES)], axis=1)


def _rwkv_pre_body(x_ref, xp_ref, xn_ref, sh_ref, sc_ref, g_ref, mu_ref, wrkv_ref,
                   w0_ref, w1_ref, w2_ref, a0_ref, a1_ref, a2_ref, g1_ref, g2_ref,
                   kk_scale_ref, ka_ref,
                   r_ref, v_ref, kk_ref, lw_ref, kd_ref, a_ref, gg_ref):
    t = pl.program_id(1)
    nt = pl.num_programs(1)
    g = g_ref[...]
    sh = sh_ref[...]
    sc = sc_ref[...]
    h = _mod_norm(x_ref[...], g, sh, sc)
    tm = h.shape[0]
    halo = xp_ref.shape[0]
    h_before = _mod_norm(xp_ref[...], g, sh, sc)[halo - 1:halo]
    h_after = _mod_norm(xn_ref[...], g, sh, sc)[0:1]
    h_before = jnp.where(t == 0, 0.0, h_before)
    h_after = jnp.where(t == nt - 1, 0.0, h_after)
    row = lax.broadcasted_iota(jnp.int32, h.shape, 0)
    prev = jnp.where(row == 0, h_before, pltpu.roll(h, 1, 0))
    nxt = jnp.where(row == tm - 1, h_after, pltpu.roll(h, tm - 1, 0))
    xx = 0.5 * (prev + nxt) - h
    mix = lambda i: (h + xx * mu_ref[i:i + 1, :]).astype(BF16)
    r = _dot(mix(0), wrkv_ref[0])
    xw = mix(1)
    k = _dot(mix(2), wrkv_ref[1])
    v = _dot(mix(3), wrkv_ref[2])
    xa = mix(4)
    xg = mix(5)
    r_ref[...] = r
    v_ref[...] = v
    kk = k * kk_scale_ref[...]
    ssq = _segment_apply(kk * kk, _head_ones(1.0))
    kk_ref[...] = kk * lax.rsqrt(ssq + KK_EPS)
    ka = ka_ref[...]
    for d in range(2):
        lora = _dot(jnp.tanh(_dot(xw, w1_ref[d])).astype(BF16), w2_ref[d])
        lw_ref[d] = -DECAY_SCALE * jax.nn.sigmoid(w0_ref[d:d + 1, :] + lora)
        a = jax.nn.sigmoid(a0_ref[d:d + 1, :] + _dot(_dot(xa, a1_ref[d]).astype(BF16), a2_ref[d]))
        a_ref[d] = a
        kd_ref[d] = k * (1.0 + (a - 1.0) * ka)
        gg_ref[d] = _dot(jax.nn.sigmoid(_dot(xg, g1_ref[d])).astype(BF16), g2_ref[d])


def _rwkv_pre(x, mod, per_batch, g, p, tm):
    b, l, d = x.shape
    halo = 8
    nh = l // halo
    tok = pl.BlockSpec((None, tm, d), lambda i, t: (i, t, 0))
    tok2 = pl.BlockSpec((2, None, tm, d), lambda i, t: (0, i, t, 0))
    before = pl.BlockSpec((None, halo, d),
                          lambda i, t: (i, jnp.maximum(t * (tm // halo) - 1, 0), 0))
    after = pl.BlockSpec((None, halo, d),
                         lambda i, t: (i, jnp.minimum((t + 1) * (tm // halo), nh - 1), 0))
    one = jax.ShapeDtypeStruct((b, l, d), F32)
    two = jax.ShapeDtypeStruct((2, b, l, d), F32)
    consts = [g.reshape(1, d), p["mu"], p["w_rkv"], p["w0"], p["w1"], p["w2"], p["a0"],
              p["a1"], p["a2"], p["g1"], p["g2"], p["k_k"].reshape(1, d),
              p["k_a"].reshape(1, d)]
    return pl.pallas_call(
        _rwkv_pre_body,
        out_shape=(one, one, one, two, two, two, two),
        grid=(b, l // tm),
        in_specs=[tok, before, after, _mod_block(d, 3, per_batch), _mod_block(d, 4, per_batch)]
                 + [_const_spec(c.shape) for c in consts],
        out_specs=(tok, tok, tok, tok2, tok2, tok2, tok2),
        compiler_params=_params("parallel", "parallel"),
        name="rwkv_pre",
    )(x, x, x, mod, mod, *consts)


def _stack_heads(x, first_head):
    return jnp.concatenate([jnp.where(first_head, x, 0.0), jnp.where(first_head, 0.0, x)],
                           axis=0)


def _wkv_chunks(r, lw, k, v, kk, ag, s_prev, ln_g, ln_b, r_k, reverse):
    each = lambda fn, *cols: [fn(*args) for args in zip(*cols)]
    chunk = r[0].shape[0]
    rows = 2 * chunk
    cat = lambda *parts: jnp.concatenate(parts, axis=0)

    ti = lax.broadcasted_iota(jnp.int32, (chunk, chunk), 0)
    si = lax.broadcasted_iota(jnp.int32, (chunk, chunk), 1)
    tri = jnp.where((si >= ti) if reverse else (si <= ti), 1.0, 0.0).astype(BF16)

    def cumulative(x):
        hi = x.astype(BF16)
        mid = (x - hi.astype(F32)).astype(BF16)
        lo = (x - hi.astype(F32) - mid.astype(F32)).astype(BF16)
        return _dot(jnp.concatenate([tri, tri, tri], axis=1), cat(hi, mid, lo))

    cum = each(cumulative, lw)
    total = [x[0:1] if reverse else x[chunk - 1:chunk] for x in cum]
    b_vec = each(lambda a, b: a * b, kk, ag)

    first_head = lax.broadcasted_iota(jnp.int32, (chunk, LANES), 1) < HEAD_DIM
    stack = lambda x: _stack_heads(x, first_head)
    lhs = each(lambda kk_, r_, cum_, lw_: cat(stack(-kk_ * jnp.exp(cum_ - lw_)),
                                              stack(r_ * jnp.exp(cum_))), kk, r, cum, lw)
    rhs = each(lambda b_, k_, cum_: cat(stack(b_ * jnp.exp(-cum_)), stack(k_ * jnp.exp(-cum_))),
               b_vec, k, cum)
    rhs_end = each(lambda b_, k_, cum_, tot_: cat(stack(b_ * jnp.exp(tot_ - cum_)),
                                                  stack(k_ * jnp.exp(tot_ - cum_))),
                   b_vec, k, cum, total)
    vb = each(stack, v)

    gram = each(_dot1_nt, lhs, rhs)
    from_state = each(_dot1_nt, lhs, s_prev)
    tt = lax.broadcasted_iota(jnp.int32, (rows, rows), 0) % chunk
    ss = lax.broadcasted_iota(jnp.int32, (rows, rows), 1) % chunk
    strict = (ss > tt) if reverse else (ss < tt)
    incl = (ss >= tt) if reverse else (ss <= tt)
    l_ab = [jnp.where(strict, g[:rows, :rows], 0.0) for g in gram]
    l_ak = [jnp.where(strict, g[:rows, rows:], 0.0) for g in gram]
    m_rb = [jnp.where(incl, g[rows:, :rows], 0.0) for g in gram]
    m_rk = [jnp.where(incl, g[rows:, rows:], 0.0) for g in gram]

    from_v = each(lambda a, b, vb_: _dot1(cat(a, b), vb_), l_ak, m_rk, vb)
    x_rhs = each(lambda a, b: a[:rows] + b[:rows], from_state, from_v)

    eye = jnp.where(lax.broadcasted_iota(jnp.int32, (rows, rows), 0)
                    == lax.broadcasted_iota(jnp.int32, (rows, rows), 1), 1.0, 0.0)
    inv = [eye + m for m in l_ab]
    power = each(_dot1, l_ab, l_ab)
    span = 2
    while 2 * span < chunk:
        both = each(lambda i_, p_: _dot1(p_, jnp.concatenate([i_, p_], axis=1)), inv, power)
        inv = each(lambda i_, b_: i_ + b_[:, :rows], inv, both)
        power = [b_[:, rows:] for b_ in both]
        span *= 2
    inv = each(lambda i_, p_: i_ + _dot1(p_, i_), inv, power)

    u = each(_dot1, inv, x_rhs)
    y_stacked = each(lambda fs, fv, m, u_: fs[rows:] + fv[rows:] + _dot1(m, u_),
                     from_state, from_v, m_rb, u)
    y = [ys[:chunk] + ys[chunk:] for ys in y_stacked]
    s_new = each(lambda s, tot_, u_, vb_, re: s * jnp.exp(tot_) + _dot3_tn(cat(u_, vb_), re),
                 s_prev, total, u, vb, rhs_end)

    avg = _head_ones(1.0 / HEAD_DIM)
    ones = _head_ones(1.0)
    mean = [_dot2_exact_rhs(y_, avg) for y_ in y]
    cen = each(lambda a, b: a - b, y, mean)
    var = [_dot2_exact_rhs(c_ * c_, avg) for c_ in cen]
    bonus = each(lambda r_, k_, rk_, v_: _dot2_exact_rhs(r_ * k_ * rk_, ones) * v_, r, k, r_k, v)
    z = each(lambda c_, var_, g_, b_, bo_: c_ * lax.rsqrt(var_ + GN_EPS) * g_ + b_ + bo_,
             cen, var, ln_g, ln_b, bonus)
    return z, s_new


def _wkv_body(r_ref, lw_ref, k_ref, v_ref, kk_ref, ag_ref, s0_ref, lng_ref, lnb_ref, rk_ref,
              z_ref, sout_ref, state, *, reverse):
    c = pl.program_id(2)
    nc = pl.num_programs(2)
    group = state.shape[0]

    @pl.when(c == 0)
    def _():
        state[...] = s0_ref[...]

    lanes = [slice(i * LANES, (i + 1) * LANES) for i in range(group)]
    pairs = lambda ref: [ref[:, ls] for ls in lanes]
    z, s_new = _wkv_chunks(pairs(r_ref), pairs(lw_ref), pairs(k_ref), pairs(v_ref),
                           pairs(kk_ref), pairs(ag_ref), [state[i] for i in range(group)],
                           pairs(lng_ref), pairs(lnb_ref), pairs(rk_ref), reverse)
    for i in range(group):
        z_ref[:, lanes[i]] = z[i]
        state[i] = s_new[i]

    @pl.when(c == nc - 1)
    def _():
        for i in range(group):
            sout_ref[i] = s_new[i]


def _wkv_scan(r, lw, kd, v, kk, ag, s0, ln_g, ln_b, r_k, direction):
    b, l, d = r.shape
    reverse = direction == 1
    group = min(SCAN_PAIRS, d // LANES)
    width = group * LANES
    nc = l // SCAN_CHUNK
    at = (lambda c: nc - 1 - c) if reverse else (lambda c: c)
    tok = pl.BlockSpec((None, SCAN_CHUNK, width), lambda i, p, c: (i, at(c), p))
    tok2 = pl.BlockSpec((None, None, SCAN_CHUNK, width), lambda i, p, c: (direction, i, at(c), p))
    st = pl.BlockSpec((None, group, LANES, LANES), lambda i, p, c: (i, p, 0, 0))
    vec = pl.BlockSpec((1, width), lambda i, p, c: (0, p))
    return pl.pallas_call(
        functools.partial(_wkv_body, reverse=reverse),
        out_shape=(jax.ShapeDtypeStruct((b, l, d), F32),
                   jax.ShapeDtypeStruct(s0.shape, F32)),
        grid=(b, d // width, nc),
        in_specs=[tok, tok2, tok2, tok, tok, tok2, st, vec, vec, vec],
        out_specs=(tok, st),
        scratch_shapes=[pltpu.VMEM((group, LANES, LANES), F32)],
        compiler_params=_params("parallel", "parallel", "arbitrary"),
        name="wkv_rev" if reverse else "wkv_fwd",
    )(r, lw, kd, v, kk, ag, s0, ln_g.reshape(1, d), ln_b.reshape(1, d), r_k.reshape(1, d))


def _pair_states(s):
    b, h, n, _ = s.shape
    s = s.reshape(b, h // 2, 2, n, n)
    z = jnp.zeros_like(s[:, :, 0])
    top = jnp.concatenate([s[:, :, 0], z], axis=-1)
    bot = jnp.concatenate([z, s[:, :, 1]], axis=-1)
    return jnp.concatenate([top, bot], axis=-2)


def _unpair_states(s):
    b, p, _, _ = s.shape
    n = HEAD_DIM
    return jnp.stack([s[:, :, :n, :n], s[:, :, n:, n:]], axis=2).reshape(b, 2 * p, n, n)


def _rwkv_post_body(zf_ref, zb_ref, gg_ref, x_ref, gt_ref, wo_ref, o_ref):
    y = (zf_ref[...] * gg_ref[0] + zb_ref[...] * gg_ref[1]).astype(BF16)
    o_ref[...] = x_ref[...] + gt_ref[...] * _dot(y, wo_ref[...])


def _rwkv_post(zf, zb, gg, x, mod, per_batch, w_o, tm):
    b, l, d = x.shape
    tok = pl.BlockSpec((None, tm, d), lambda i, t: (i, t, 0))
    tok2 = pl.BlockSpec((2, None, tm, d), lambda i, t: (0, i, t, 0))
    return pl.pallas_call(
        _rwkv_post_body,
        out_shape=jax.ShapeDtypeStruct(x.shape, F32),
        grid=(b, l // tm),
        in_specs=[tok, tok, tok2, tok, _mod_block(d, 5, per_batch), _const_spec(w_o.shape)],
        out_specs=tok,
        compiler_params=_params("parallel", "parallel"),
        name="rwkv_post",
    )(zf, zb, gg, x, mod, w_o)


def _final_norm_body(x_ref, g_ref, o_ref):
    x = x_ref[...]
    ms = jnp.mean(x * x, axis=-1, keepdims=True)
    o_ref[...] = x * lax.rsqrt(ms + RMS_EPS) * g_ref[...]


def _final_norm(x, g, tm):
    b, l, d = x.shape
    tok = pl.BlockSpec((None, tm, d), lambda i, t: (i, t, 0))
    return pl.pallas_call(
        _final_norm_body,
        out_shape=jax.ShapeDtypeStruct(x.shape, F32),
        grid=(b, l // tm),
        in_specs=[tok, _const_spec((1, d))],
        out_specs=tok,
        compiler_params=_params("parallel", "parallel"),
        name="final_norm",
    )(x, g.reshape(1, d))


def _token_tile(l):
    return 256 if l % 256 == 0 else 128


def kernel(x_prompt, x_sample, cache_k, cache_v, state_wkv, c, c_ctx, ada_w, ada_b, norm_g, ffn_w_in, ffn_w_out, attn_w_qkv, attn_w_o, attn_sink, rwkv_mu, rwkv_w_rkv, rwkv_w0, rwkv_w1, rwkv_w2, rwkv_a0, rwkv_a1, rwkv_a2, rwkv_g1, rwkv_g2, rwkv_k_k, rwkv_k_a, rwkv_r_k, rwkv_ln_g, rwkv_ln_b, rwkv_w_o, norm_f):
    depth = ada_w.shape[0]
    bp, sp, d = x_prompt.shape
    bs, ls, _ = x_sample.shape
    n_heads = d // HEAD_DIM
    kv_dim = (n_heads // GQA_GROUP) * HEAD_DIM
    tm_p = _token_tile(sp)
    tm_s = _token_tile(ls)

    rows = -(-(1 + bs) // 8) * 8
    cond = jnp.zeros((rows, d), F32).at[0].set(c_ctx).at[1:1 + bs].set(c)
    mods = _ada(cond, ada_w, ada_b)

    ffn_w_in_b = ffn_w_in.astype(BF16)
    ffn_w_out_b = ffn_w_out.astype(BF16)
    rope_tables = _rope_tables(ls, GRID_W)

    xp, xs = x_prompt, x_sample
    ctx_k, ctx_v, ctx_state = [], [], []
    for l in range(depth):
        mod_p = mods[l, 0:1].reshape(1, N_ADA, 1, d)
        mod_s = mods[l, 1:1 + bs].reshape(bs, N_ADA, 1, d)
        xp = _ffn(xp, mod_p, (0, 1, 2), False, norm_g[l, 0], ffn_w_in_b[l, 0], ffn_w_out_b[l, 0], tm_p)
        xs = _ffn(xs, mod_s, (0, 1, 2), True, norm_g[l, 0], ffn_w_in_b[l, 0], ffn_w_out_b[l, 0], tm_s)
        i = l // 2
        if l % 2 == 0:
            w_qkv = attn_w_qkv[i].astype(BF16)
            w_o = attn_w_o[i].astype(BF16)
            q, k, v = _qkv(xp, mod_p, False, norm_g[l, 1], w_qkv, None, tm_p)
            ctx_k.append(k.reshape(bp, sp, kv_dim // HEAD_DIM, HEAD_DIM))
            ctx_v.append(v.reshape(bp, sp, kv_dim // HEAD_DIM, HEAD_DIM))
            xp = _attn_ctx(q, k, v, attn_sink[i], xp, mod_p, w_o)
            q, k, v = _qkv(xs, mod_s, True, norm_g[l, 1], w_qkv, rope_tables, tm_s)
            past = cache_k.shape[2]
            xs = _attn_lat(q, k, v, cache_k[:, i].reshape(bs, past, kv_dim),
                           cache_v[:, i].reshape(bs, past, kv_dim), attn_sink[i], xs, mod_s, w_o)
        else:
            p = dict(mu=rwkv_mu[i], w_rkv=rwkv_w_rkv[i].astype(BF16), w0=rwkv_w0[i],
                     w1=rwkv_w1[i].astype(BF16), w2=rwkv_w2[i].astype(BF16), a0=rwkv_a0[i],
                     a1=rwkv_a1[i].astype(BF16), a2=rwkv_a2[i].astype(BF16),
                     g1=rwkv_g1[i].astype(BF16), g2=rwkv_g2[i].astype(BF16),
                     k_k=rwkv_k_k[i], k_a=rwkv_k_a[i])
            w_o = rwkv_w_o[i].astype(BF16)
            r_k = rwkv_r_k[i].reshape(2, d)
            zero_state = jnp.zeros((bp, n_heads // 2, LANES, LANES), F32)
            new_states = []
            for x, mod, per_batch, tm, is_prompt in ((xp, mod_p, False, tm_p, True),
                                                     (xs, mod_s, True, tm_s, False)):
                r, v, kk, lw, kd, ag, gg = _rwkv_pre(x, mod, per_batch, norm_g[l, 1], p, tm)
                zs = []
                for dr in range(2):
                    s0 = zero_state if is_prompt else _pair_states(state_wkv[:, i, dr])
                    z, s_end = _wkv_scan(r, lw, kd, v, kk, ag, s0,
                                         rwkv_ln_g[i, dr], rwkv_ln_b[i, dr], r_k[dr], dr)
                    zs.append(z)
                    if is_prompt:
                        new_states.append(_unpair_states(s_end))
                x = _rwkv_post(zs[0], zs[1], gg, x, mod, per_batch, w_o, tm)
                if is_prompt:
                    xp = x
                else:
                    xs = x
            ctx_state.append(jnp.stack(new_states, axis=1))
        xp = _ffn(xp, mod_p, (6, 7, 8), False, norm_g[l, 2], ffn_w_in_b[l, 1], ffn_w_out_b[l, 1], tm_p)
        xs = _ffn(xs, mod_s, (6, 7, 8), True, norm_g[l, 2], ffn_w_in_b[l, 1], ffn_w_out_b[l, 1], tm_s)

    y_prompt = _final_norm(xp, norm_f, tm_p)
    y_sample = _final_norm(xs, norm_f, tm_s)
    return (y_prompt, y_sample, jnp.stack(ctx_k, axis=1), jnp.stack(ctx_v, axis=1),
            jnp.stack(ctx_state, axis=1))
```

```python
import functools

import jax
import jax.numpy as jnp
from jax import lax
from jax.experimental import pallas as pl
from jax.experimental.pallas import tpu as pltpu

F32 = jnp.float32
BF16 = jnp.bfloat16

HEAD_DIM = 64
GQA_GROUP = 4
LANES = 128
N_ADA = 9
ATTN_BLK = 128
GRID_W = 64
ROPE_BASE = 10000.0
ROPE_PAIRS = HEAD_DIM // 4
RMS_EPS = 1e-6
GN_EPS = 64e-5
KK_EPS = 1e-12
DECAY_SCALE = 0.6065306597126334
NEG_INF = -1e30
LOG2_E = 1.4426950408889634
SCAN_CHUNK = 64
SCAN_PAIRS = 8
VMEM_LIMIT = 56 * 1024 * 1024


def _params(*sem):
    return pltpu.CompilerParams(dimension_semantics=sem, vmem_limit_bytes=VMEM_LIMIT)


def _const_spec(shape):
    nd = len(shape)
    return pl.BlockSpec(shape, lambda *_: (0,) * nd, pipeline_mode=pl.Buffered(1))


def _mod_block(d_model, chunk, per_batch):
    def index(b, *_):
        return (b if per_batch else 0, chunk, 0, 0)
    return pl.BlockSpec((None, None, 1, d_model), index)


def _mod_norm(x, g, shift, scale):
    ms = jnp.mean(x * x, axis=-1, keepdims=True)
    return (x * lax.rsqrt(ms + RMS_EPS) * g) * (1.0 + scale) + shift


def _dot(a, b):
    return jnp.dot(a, b, preferred_element_type=F32)


def _dot_nt(a, b):
    return lax.dot_general(a, b, (((1,), (1,)), ((), ())), preferred_element_type=F32)


def _dot_tn(a, b):
    return lax.dot_general(a, b, (((0,), (0,)), ((), ())), preferred_element_type=F32)


def _split(a):
    hi = a.astype(BF16)
    lo = (a - hi.astype(F32)).astype(BF16)
    return hi, lo


def _dot1(a, b):
    return _dot(a.astype(BF16), b.astype(BF16))


def _dot1_nt(a, b):
    return _dot_nt(a.astype(BF16), b.astype(BF16))


def _dot3_tn(a, b):
    ah, al = _split(a)
    bh, bl = _split(b)
    return _dot_tn(ah, bh) + (_dot_tn(ah, bl) + _dot_tn(al, bh))


def _dot2_exact_rhs(a, b_bf16):
    ah, al = _split(a)
    return _dot(jnp.concatenate([ah, al], axis=1), jnp.concatenate([b_bf16, b_bf16], axis=0))


def _head_ones(scale):
    r = lax.broadcasted_iota(jnp.int32, (LANES, LANES), 0) // HEAD_DIM
    c = lax.broadcasted_iota(jnp.int32, (LANES, LANES), 1) // HEAD_DIM
    return jnp.where(r == c, scale, 0.0).astype(BF16)


def _ada_body(c_ref, w_ref, b_ref, o_ref):
    c = c_ref[...]
    s = (c * jax.nn.sigmoid(c)).astype(BF16)
    o_ref[...] = _dot(s, w_ref[...].astype(BF16)) + b_ref[...]


def _ada(cond, ada_w, ada_b):
    depth, d_model, n_out = ada_w.shape
    rows = cond.shape[0]
    tn = d_model
    return pl.pallas_call(
        _ada_body,
        out_shape=jax.ShapeDtypeStruct((depth, rows, n_out), F32),
        grid=(depth, n_out // tn),
        in_specs=[
            pl.BlockSpec((rows, d_model), lambda l, n: (0, 0)),
            pl.BlockSpec((None, d_model, tn), lambda l, n: (l, 0, n)),
            pl.BlockSpec((None, 1, tn), lambda l, n: (l, 0, n)),
        ],
        out_specs=pl.BlockSpec((None, rows, tn), lambda l, n: (l, 0, n)),
        compiler_params=_params("parallel", "parallel"),
        name="ada",
    )(cond, ada_w, ada_b.reshape(depth, 1, n_out))


def _ffn_body(x_ref, sh_ref, sc_ref, gt_ref, g_ref, win_ref, wout_ref, o_ref):
    x = x_ref[...]
    h = _mod_norm(x, g_ref[...], sh_ref[...], sc_ref[...]).astype(BF16)
    hh = _dot(h, win_ref[...])
    d_ff = hh.shape[1] // 2
    gate = hh[:, :d_ff]
    act = (gate * jax.nn.sigmoid(gate) * hh[:, d_ff:]).astype(BF16)
    y = _dot(act, wout_ref[...])
    o_ref[...] = x + (0.5 * gt_ref[...]) * y


def _ffn(x, mod, chunks, per_batch, g, w_in, w_out, tm):
    b, l, d = x.shape
    tok = pl.BlockSpec((None, tm, d), lambda i, t: (i, t, 0))
    return pl.pallas_call(
        _ffn_body,
        out_shape=jax.ShapeDtypeStruct(x.shape, F32),
        grid=(b, l // tm),
        in_specs=[
            tok,
            _mod_block(d, chunks[0], per_batch),
            _mod_block(d, chunks[1], per_batch),
            _mod_block(d, chunks[2], per_batch),
            _const_spec((1, d)),
            _const_spec(w_in.shape),
            _const_spec(w_out.shape),
        ],
        out_specs=tok,
        compiler_params=_params("parallel", "parallel"),
        name="ffn",
    )(x, mod, mod, mod, g.reshape(1, d), w_in, w_out)


def _rope(x, cos, sin_signed):
    width = x.shape[1]
    lane = lax.broadcasted_iota(jnp.int32, x.shape, 1)
    first_half = (lane % (2 * ROPE_PAIRS)) < ROPE_PAIRS
    partner = jnp.where(first_half,
                        pltpu.roll(x, width - ROPE_PAIRS, 1),
                        pltpu.roll(x, ROPE_PAIRS, 1))
    return x * cos + partner * sin_signed


def _qkv_body(*refs, rope, d_model, kv_dim):
    if rope:
        x_ref, sh_ref, sc_ref, g_ref, w_ref, cos_ref, sin_ref, q_ref, k_ref, v_ref = refs
    else:
        x_ref, sh_ref, sc_ref, g_ref, w_ref, q_ref, k_ref, v_ref = refs
    h = _mod_norm(x_ref[...], g_ref[...], sh_ref[...], sc_ref[...]).astype(BF16)
    qkv = _dot(h, w_ref[...])
    q = qkv[:, :d_model]
    k = qkv[:, d_model:d_model + kv_dim]
    v = qkv[:, d_model + kv_dim:]
    if rope:
        cos = cos_ref[...]
        sin = sin_ref[...]
        q = _rope(q, jnp.concatenate([cos] * (d_model // LANES), axis=1),
                  jnp.concatenate([sin] * (d_model // LANES), axis=1))
        k = _rope(k, jnp.concatenate([cos] * (kv_dim // LANES), axis=1),
                  jnp.concatenate([sin] * (kv_dim // LANES), axis=1))
    if rope:
        q = (q * (HEAD_DIM ** -0.5 * LOG2_E)).astype(BF16)
        for hd in range(d_model // HEAD_DIM):
            q_ref[hd] = q[:, hd * HEAD_DIM:(hd + 1) * HEAD_DIM]
        for hd in range(kv_dim // HEAD_DIM):
            k_ref[hd] = k[:, hd * HEAD_DIM:(hd + 1) * HEAD_DIM].astype(BF16)
            v_ref[hd] = v[:, hd * HEAD_DIM:(hd + 1) * HEAD_DIM].astype(BF16)
    else:
        q_ref[...] = (q * (HEAD_DIM ** -0.5)).astype(BF16)
        k_ref[...] = k
        v_ref[...] = v


def _qkv(x, mod, per_batch, g, w_qkv, rope_tables, tm):
    b, l, d = x.shape
    kv_dim = (w_qkv.shape[1] - d) // 2
    rope = rope_tables is not None
    tok = lambda w: pl.BlockSpec((None, tm, w), lambda i, t: (i, t, 0))
    heads = lambda w: pl.BlockSpec((None, w // HEAD_DIM, tm, HEAD_DIM), lambda i, t: (i, 0, t, 0))
    head_shape = lambda w: jax.ShapeDtypeStruct((b, w // HEAD_DIM, l, HEAD_DIM), BF16)
    in_specs = [tok(d), _mod_block(d, 3, per_batch), _mod_block(d, 4, per_batch),
                _const_spec((1, d)), _const_spec(w_qkv.shape)]
    args = [x, mod, mod, g.reshape(1, d), w_qkv]
    if rope:
        tab = pl.BlockSpec((tm, LANES), lambda i, t: (t, 0))
        in_specs += [tab, tab]
        args += list(rope_tables)
    return pl.pallas_call(
        functools.partial(_qkv_body, rope=rope, d_model=d, kv_dim=kv_dim),
        out_shape=((head_shape(d), head_shape(kv_dim), head_shape(kv_dim)) if rope else
                   (jax.ShapeDtypeStruct((b, l, d), BF16),
                    jax.ShapeDtypeStruct((b, l, kv_dim), F32),
                    jax.ShapeDtypeStruct((b, l, kv_dim), F32))),
        grid=(b, l // tm),
        in_specs=in_specs,
        out_specs=((heads(d), heads(kv_dim), heads(kv_dim)) if rope else
                   (tok(d), tok(kv_dim), tok(kv_dim))),
        compiler_params=_params("parallel", "parallel"),
        name="qkv_rope" if rope else "qkv",
    )(*args)


def _rope_tables(l, grid_w):
    pos = jnp.arange(l, dtype=jnp.int32)
    row = (pos // grid_w).astype(F32)
    col = (pos % grid_w).astype(F32)
    inv = jnp.power(ROPE_BASE, -jnp.arange(ROPE_PAIRS, dtype=F32) / ROPE_PAIRS)
    ang_r = row[:, None] * inv
    ang_c = col[:, None] * inv
    cos = jnp.concatenate([jnp.cos(ang_r)] * 2 + [jnp.cos(ang_c)] * 2, axis=1)
    sin = jnp.concatenate([-jnp.sin(ang_r), jnp.sin(ang_r),
                           -jnp.sin(ang_c), jnp.sin(ang_c)], axis=1)
    return jnp.concatenate([cos, cos], axis=1), jnp.concatenate([sin, sin], axis=1)


def _softmax_pv(scores, values, sink):
    m = jnp.maximum(functools.reduce(
        jnp.maximum, [jnp.max(s, axis=1, keepdims=True) for s in scores]), sink)
    den = jnp.exp(sink - m)
    acc = None
    for s, v in zip(scores, values):
        p = jnp.exp(s - m)
        den = den + jnp.sum(p, axis=1, keepdims=True)
        pv = _dot(p.astype(BF16), v)
        acc = pv if acc is None else acc + pv
    return acc / den


def _attn_ctx_body(sink_ref, q_ref, k_ref, v_ref, x_ref, gt_ref, wo_ref, o_ref, head_out):
    n_heads = q_ref.shape[1] // HEAD_DIM
    for kv in range(n_heads // GQA_GROUP):
        ks = slice(kv * HEAD_DIM, (kv + 1) * HEAD_DIM)
        kb = k_ref[:, ks].astype(BF16)
        vb = v_ref[:, ks].astype(BF16)
        for grp in range(GQA_GROUP):
            hd = kv * GQA_GROUP + grp
            hs = slice(hd * HEAD_DIM, (hd + 1) * HEAD_DIM)
            s = _dot_nt(q_ref[:, hs], kb)
            head_out[:, hs] = _softmax_pv([s], [vb], sink_ref[hd]).astype(BF16)
    y = _dot(head_out[...], wo_ref[...])
    o_ref[...] = x_ref[...] + gt_ref[...] * y


def _attn_ctx(q, k, v, sink, x, mod, w_o):
    b, s, d = x.shape
    kv_dim = k.shape[2]
    tok = lambda w: pl.BlockSpec((None, s, w), lambda i: (i, 0, 0))
    return pl.pallas_call(
        _attn_ctx_body,
        out_shape=jax.ShapeDtypeStruct(x.shape, F32),
        grid=(b,),
        in_specs=[pl.BlockSpec(memory_space=pltpu.SMEM),
                  tok(d), tok(kv_dim), tok(kv_dim), tok(d),
                  _mod_block(d, 5, False), _const_spec(w_o.shape)],
        out_specs=tok(d),
        scratch_shapes=[pltpu.VMEM((s, d), BF16)],
        compiler_params=_params("parallel"),
        name="attn_ctx",
    )(sink, q, k, v, x, mod, w_o)


def _attn_lat_body(sink_ref, q_ref, k_ref, v_ref, kc_ref, vc_ref, x_ref, gt_ref, wo_ref,
                   o_ref, heads_t, bias):
    n_heads, blk, _ = q_ref.shape
    seq = k_ref.shape[1]
    span = 3 * blk
    width = GQA_GROUP * blk
    j = pl.program_id(1)
    start = pl.multiple_of(jnp.clip((j - 1) * blk, 0, seq - span), blk)
    kpos = start + lax.broadcasted_iota(jnp.int32, (span, blk), 0)
    qpos = j * blk + lax.broadcasted_iota(jnp.int32, (span, blk), 1)
    bias[...] = jnp.where(jnp.abs(qpos - kpos) <= blk, 0.0, NEG_INF)
    head_of_col = lax.broadcasted_iota(jnp.int32, (1, width), 1) // blk
    for kv in range(n_heads // GQA_GROUP):
        first = kv * GQA_GROUP
        q4 = q_ref[first:first + GQA_GROUP].reshape(width, HEAD_DIM)
        kl = k_ref[kv, pl.ds(start, span), :]
        vl = v_ref[kv, pl.ds(start, span), :]
        s_loc = _dot_nt(kl, q4) + jnp.concatenate([bias[...]] * GQA_GROUP, axis=1)
        s_ctx = _dot_nt(kc_ref[kv], q4)
        sink = jnp.zeros((1, width), F32)
        for grp in range(GQA_GROUP):
            sink = jnp.where(head_of_col == grp, sink_ref[first + grp] * LOG2_E, sink)
        m = jnp.maximum(jnp.maximum(jnp.max(s_loc, axis=0, keepdims=True),
                                    jnp.max(s_ctx, axis=0, keepdims=True)), sink)
        p_loc = jnp.exp2(s_loc - m)
        p_ctx = jnp.exp2(s_ctx - m)
        den = (jnp.sum(p_loc, axis=0, keepdims=True) + jnp.sum(p_ctx, axis=0, keepdims=True)
               + jnp.exp2(sink - m))
        out_t = (_dot_tn(vl, p_loc.astype(BF16)) + _dot_tn(vc_ref[kv], p_ctx.astype(BF16))) / den
        for grp in range(GQA_GROUP):
            hd = first + grp
            heads_t[hd * HEAD_DIM:(hd + 1) * HEAD_DIM, :] = (
                out_t[:, grp * blk:(grp + 1) * blk].astype(BF16))
    y = _dot_tn(heads_t[...], wo_ref[...])
    o_ref[...] = x_ref[...] + gt_ref[...] * y


def _attn_lat(q, k, v, kc, vc, sink, x, mod, w_o):
    b, l, d = x.shape
    n_heads, n_kv, past = q.shape[1], k.shape[1], kc.shape[2]
    assert l % ATTN_BLK == 0 and l >= 3 * ATTN_BLK
    blk = pl.BlockSpec((None, ATTN_BLK, d), lambda i, j: (i, j, 0))
    full = lambda n: pl.BlockSpec((None, n_kv, n, HEAD_DIM), lambda i, j: (i, 0, 0, 0))
    return pl.pallas_call(
        _attn_lat_body,
        out_shape=jax.ShapeDtypeStruct(x.shape, F32),
        grid=(b, l // ATTN_BLK),
        in_specs=[pl.BlockSpec(memory_space=pltpu.SMEM),
                  pl.BlockSpec((None, n_heads, ATTN_BLK, HEAD_DIM), lambda i, j: (i, 0, j, 0)),
                  full(l), full(l), full(past), full(past), blk,
                  _mod_block(d, 5, True), _const_spec(w_o.shape)],
        out_specs=blk,
        scratch_shapes=[pltpu.VMEM((d, ATTN_BLK), BF16),
                        pltpu.VMEM((3 * ATTN_BLK, ATTN_BLK), F32)],
        compiler_params=_params("parallel", "arbitrary"),
        name="attn_lat",
    )(sink, q, k, v, kc, vc, x, mod, w_o)


def _segment_apply(z, mat_bf16):
    return jnp.concatenate(
        [_dot2_exact_rhs(z[:, p * LANES:(p + 1) * LANES], mat_bf16)
         for p in range(z.shape[1] // LANES)], axis=1)


def _rwkv_pre_body(x_ref, xp_ref, xn_ref, sh_ref, sc_ref, g_ref, mu_ref, wrkv_ref,
                   w0_ref, w1_ref, w2_ref, a0_ref, a1_ref, a2_ref, g1_ref, g2_ref,
                   kk_scale_ref, ka_ref,
                   r_ref, v_ref, kk_ref, lw_ref, kd_ref, a_ref, gg_ref):
    t = pl.program_id(1)
    nt = pl.num_programs(1)
    g = g_ref[...]
    sh = sh_ref[...]
    sc = sc_ref[...]
    h = _mod_norm(x_ref[...], g, sh, sc)
    tm = h.shape[0]
    halo = xp_ref.shape[0]
    h_before = _mod_norm(xp_ref[...], g, sh, sc)[halo - 1:halo]
    h_after = _mod_norm(xn_ref[...], g, sh, sc)[0:1]
    h_before = jnp.where(t == 0, 0.0, h_before)
    h_after = jnp.where(t == nt - 1, 0.0, h_after)
    row = lax.broadcasted_iota(jnp.int32, h.shape, 0)
    prev = jnp.where(row == 0, h_before, pltpu.roll(h, 1, 0))
    nxt = jnp.where(row == tm - 1, h_after, pltpu.roll(h, tm - 1, 0))
    xx = 0.5 * (prev + nxt) - h
    mix = lambda i: (h + xx * mu_ref[i:i + 1, :]).astype(BF16)
    r = _dot(mix(0), wrkv_ref[0])
    xw = mix(1)
    k = _dot(mix(2), wrkv_ref[1])
    v = _dot(mix(3), wrkv_ref[2])
    xa = mix(4)
    xg = mix(5)
    r_ref[...] = r
    v_ref[...] = v
    kk = k * kk_scale_ref[...]
    ssq = _segment_apply(kk * kk, _head_ones(1.0))
    kk_ref[...] = kk * lax.rsqrt(ssq + KK_EPS)
    ka = ka_ref[...]
    for d in range(2):
        lora = _dot(jnp.tanh(_dot(xw, w1_ref[d])).astype(BF16), w2_ref[d])
        lw_ref[d] = -DECAY_SCALE * jax.nn.sigmoid(w0_ref[d:d + 1, :] + lora)
        a = jax.nn.sigmoid(a0_ref[d:d + 1, :] + _dot(_dot(xa, a1_ref[d]).astype(BF16), a2_ref[d]))
        a_ref[d] = a
        kd_ref[d] = k * (1.0 + (a - 1.0) * ka)
        gg_ref[d] = _dot(jax.nn.sigmoid(_dot(xg, g1_ref[d])).astype(BF16), g2_ref[d])


def _rwkv_pre(x, mod, per_batch, g, p, tm):
    b, l, d = x.shape
    halo = 8
    nh = l // halo
    tok = pl.BlockSpec((None, tm, d), lambda i, t: (i, t, 0))
    tok2 = pl.BlockSpec((2, None, tm, d), lambda i, t: (0, i, t, 0))
    before = pl.BlockSpec((None, halo, d),
                          lambda i, t: (i, jnp.maximum(t * (tm // halo) - 1, 0), 0))
    after = pl.BlockSpec((None, halo, d),
                         lambda i, t: (i, jnp.minimum((t + 1) * (tm // halo), nh - 1), 0))
    one = jax.ShapeDtypeStruct((b, l, d), F32)
    two = jax.ShapeDtypeStruct((2, b, l, d), F32)
    consts = [g.reshape(1, d), p["mu"], p["w_rkv"], p["w0"], p["w1"], p["w2"], p["a0"],
              p["a1"], p["a2"], p["g1"], p["g2"], p["k_k"].reshape(1, d),
              p["k_a"].reshape(1, d)]
    return pl.pallas_call(
        _rwkv_pre_body,
        out_shape=(one, one, one, two, two, two, two),
        grid=(b, l // tm),
        in_specs=[tok, before, after, _mod_block(d, 3, per_batch), _mod_block(d, 4, per_batch)]
                 + [_const_spec(c.shape) for c in consts],
        out_specs=(tok, tok, tok, tok2, tok2, tok2, tok2),
        compiler_params=_params("parallel", "parallel"),
        name="rwkv_pre",
    )(x, x, x, mod, mod, *consts)


def _stack_heads(x, first_head):
    return jnp.concatenate([jnp.where(first_head, x, 0.0), jnp.where(first_head, 0.0, x)],
                           axis=0)


def _wkv_chunks(r, lw, k, v, kk, ag, s_prev, ln_g, ln_b, r_k, reverse):
    each = lambda fn, *cols: [fn(*args) for args in zip(*cols)]
    chunk = r[0].shape[0]
    rows = 2 * chunk
    cat = lambda *parts: jnp.concatenate(parts, axis=0)

    ti = lax.broadcasted_iota(jnp.int32, (chunk, chunk), 0)
    si = lax.broadcasted_iota(jnp.int32, (chunk, chunk), 1)
    tri = jnp.where((si >= ti) if reverse else (si <= ti), 1.0, 0.0).astype(BF16)

    def cumulative(x):
        hi = x.astype(BF16)
        mid = (x - hi.astype(F32)).astype(BF16)
        lo = (x - hi.astype(F32) - mid.astype(F32)).astype(BF16)
        return _dot(jnp.concatenate([tri, tri, tri], axis=1), cat(hi, mid, lo))

    cum = each(cumulative, lw)
    total = [x[0:1] if reverse else x[chunk - 1:chunk] for x in cum]
    b_vec = each(lambda a, b: a * b, kk, ag)

    first_head = lax.broadcasted_iota(jnp.int32, (chunk, LANES), 1) < HEAD_DIM
    stack = lambda x: _stack_heads(x, first_head)
    lhs = each(lambda kk_, r_, cum_, lw_: cat(stack(-kk_ * jnp.exp(cum_ - lw_)),
                                              stack(r_ * jnp.exp(cum_))), kk, r, cum, lw)
    rhs = each(lambda b_, k_, cum_: cat(stack(b_ * jnp.exp(-cum_)), stack(k_ * jnp.exp(-cum_))),
               b_vec, k, cum)
    rhs_end = each(lambda b_, k_, cum_, tot_: cat(stack(b_ * jnp.exp(tot_ - cum_)),
                                                  stack(k_ * jnp.exp(tot_ - cum_))),
                   b_vec, k, cum, total)
    vb = each(stack, v)

    gram = each(_dot1_nt, lhs, rhs)
    from_state = each(_dot1_nt, lhs, s_prev)
    tt = lax.broadcasted_iota(jnp.int32, (rows, rows), 0) % chunk
    ss = lax.broadcasted_iota(jnp.int32, (rows, rows), 1) % chunk
    strict = (ss > tt) if reverse else (ss < tt)
    incl = (ss >= tt) if reverse else (ss <= tt)
    l_ab = [jnp.where(strict, g[:rows, :rows], 0.0) for g in gram]
    l_ak = [jnp.where(strict, g[:rows, rows:], 0.0) for g in gram]
    m_rb = [jnp.where(incl, g[rows:, :rows], 0.0) for g in gram]
    m_rk = [jnp.where(incl, g[rows:, rows:], 0.0) for g in gram]

    from_v = each(lambda a, b, vb_: _dot1(cat(a, b), vb_), l_ak, m_rk, vb)
    x_rhs = each(lambda a, b: a[:rows] + b[:rows], from_state, from_v)

    eye = jnp.where(lax.broadcasted_iota(jnp.int32, (rows, rows), 0)
                    == lax.broadcasted_iota(jnp.int32, (rows, rows), 1), 1.0, 0.0)
    inv = [eye + m for m in l_ab]
    power = each(_dot1, l_ab, l_ab)
    span = 2
    while 2 * span < chunk:
        both = each(lambda i_, p_: _dot1(p_, jnp.concatenate([i_, p_], axis=1)), inv, power)
        inv = each(lambda i_, b_: i_ + b_[:, :rows], inv, both)
        power = [b_[:, rows:] for b_ in both]
        span *= 2
    inv = each(lambda i_, p_: i_ + _dot1(p_, i_), inv, power)

    u = each(_dot1, inv, x_rhs)
    y_stacked = each(lambda fs, fv, m, u_: fs[rows:] + fv[rows:] + _dot1(m, u_),
                     from_state, from_v, m_rb, u)
    y = [ys[:chunk] + ys[chunk:] for ys in y_stacked]
    s_new = each(lambda s, tot_, u_, vb_, re: s * jnp.exp(tot_) + _dot3_tn(cat(u_, vb_), re),
                 s_prev, total, u, vb, rhs_end)

    avg = _head_ones(1.0 / HEAD_DIM)
    ones = _head_ones(1.0)
    mean = [_dot2_exact_rhs(y_, avg) for y_ in y]
    cen = each(lambda a, b: a - b, y, mean)
    var = [_dot2_exact_rhs(c_ * c_, avg) for c_ in cen]
    bonus = each(lambda r_, k_, rk_, v_: _dot2_exact_rhs(r_ * k_ * rk_, ones) * v_, r, k, r_k, v)
    z = each(lambda c_, var_, g_, b_, bo_: c_ * lax.rsqrt(var_ + GN_EPS) * g_ + b_ + bo_,
             cen, var, ln_g, ln_b, bonus)
    return z, s_new


def _wkv_body(r_ref, lw_ref, k_ref, v_ref, kk_ref, ag_ref, s0_ref, lng_ref, lnb_ref, rk_ref,
              z_ref, sout_ref, state, *, reverse):
    c = pl.program_id(2)
    nc = pl.num_programs(2)
    group = state.shape[0]

    @pl.when(c == 0)
    def _():
        state[...] = s0_ref[...]

    lanes = [slice(i * LANES, (i + 1) * LANES) for i in range(group)]
    pairs = lambda ref: [ref[:, ls] for ls in lanes]
    z, s_new = _wkv_chunks(pairs(r_ref), pairs(lw_ref), pairs(k_ref), pairs(v_ref),
                           pairs(kk_ref), pairs(ag_ref), [state[i] for i in range(group)],
                           pairs(lng_ref), pairs(lnb_ref), pairs(rk_ref), reverse)
    for i in range(group):
        z_ref[:, lanes[i]] = z[i]
        state[i] = s_new[i]

    @pl.when(c == nc - 1)
    def _():
        for i in range(group):
            sout_ref[i] = s_new[i]


def _wkv_scan(r, lw, kd, v, kk, ag, s0, ln_g, ln_b, r_k, direction):
    b, l, d = r.shape
    reverse = direction == 1
    group = min(SCAN_PAIRS, d // LANES)
    width = group * LANES
    nc = l // SCAN_CHUNK
    at = (lambda c: nc - 1 - c) if reverse else (lambda c: c)
    tok = pl.BlockSpec((None, SCAN_CHUNK, width), lambda i, p, c: (i, at(c), p))
    tok2 = pl.BlockSpec((None, None, SCAN_CHUNK, width), lambda i, p, c: (direction, i, at(c), p))
    st = pl.BlockSpec((None, group, LANES, LANES), lambda i, p, c: (i, p, 0, 0))
    vec = pl.BlockSpec((1, width), lambda i, p, c: (0, p))
    return pl.pallas_call(
        functools.partial(_wkv_body, reverse=reverse),
        out_shape=(jax.ShapeDtypeStruct((b, l, d), F32),
                   jax.ShapeDtypeStruct(s0.shape, F32)),
        grid=(b, d // width, nc),
        in_specs=[tok, tok2, tok2, tok, tok, tok2, st, vec, vec, vec],
        out_specs=(tok, st),
        scratch_shapes=[pltpu.VMEM((group, LANES, LANES), F32)],
        compiler_params=_params("parallel", "parallel", "arbitrary"),
        name="wkv_rev" if reverse else "wkv_fwd",
    )(r, lw, kd, v, kk, ag, s0, ln_g.reshape(1, d), ln_b.reshape(1, d), r_k.reshape(1, d))


def _pair_states(s):
    b, h, n, _ = s.shape
    s = s.reshape(b, h // 2, 2, n, n)
    z = jnp.zeros_like(s[:, :, 0])
    top = jnp.concatenate([s[:, :, 0], z], axis=-1)
    bot = jnp.concatenate([z, s[:, :, 1]], axis=-1)
    return jnp.concatenate([top, bot], axis=-2)


def _unpair_states(s):
    b, p, _, _ = s.shape
    n = HEAD_DIM
    return jnp.stack([s[:, :, :n, :n], s[:, :, n:, n:]], axis=2).reshape(b, 2 * p, n, n)


def _rwkv_post_body(zf_ref, zb_ref, gg_ref, x_ref, gt_ref, wo_ref, o_ref):
    y = (zf_ref[...] * gg_ref[0] + zb_ref[...] * gg_ref[1]).astype(BF16)
    o_ref[...] = x_ref[...] + gt_ref[...] * _dot(y, wo_ref[...])


def _rwkv_post(zf, zb, gg, x, mod, per_batch, w_o, tm):
    b, l, d = x.shape
    tok = pl.BlockSpec((None, tm, d), lambda i, t: (i, t, 0))
    tok2 = pl.BlockSpec((2, None, tm, d), lambda i, t: (0, i, t, 0))
    return pl.pallas_call(
        _rwkv_post_body,
        out_shape=jax.ShapeDtypeStruct(x.shape, F32),
        grid=(b, l // tm),
        in_specs=[tok, tok, tok2, tok, _mod_block(d, 5, per_batch), _const_spec(w_o.shape)],
        out_specs=tok,
        compiler_params=_params("parallel", "parallel"),
        name="rwkv_post",
    )(zf, zb, gg, x, mod, w_o)


def _final_norm_body(x_ref, g_ref, o_ref):
    x = x_ref[...]
    ms = jnp.mean(x * x, axis=-1, keepdims=True)
    o_ref[...] = x * lax.rsqrt(ms + RMS_EPS) * g_ref[...]


def _final_norm(x, g, tm):
    b, l, d = x.shape
    tok = pl.BlockSpec((None, tm, d), lambda i, t: (i, t, 0))
    return pl.pallas_call(
        _final_norm_body,
        out_shape=jax.ShapeDtypeStruct(x.shape, F32),
        grid=(b, l // tm),
        in_specs=[tok, _const_spec((1, d))],
        out_specs=tok,
        compiler_params=_params("parallel", "parallel"),
        name="final_norm",
    )(x, g.reshape(1, d))


def _token_tile(l):
    return 256 if l % 256 == 0 else 128


def kernel(x_prompt, x_sample, cache_k, cache_v, state_wkv, c, c_ctx, ada_w, ada_b, norm_g, ffn_w_in, ffn_w_out, attn_w_qkv, attn_w_o, attn_sink, rwkv_mu, rwkv_w_rkv, rwkv_w0, rwkv_w1, rwkv_w2, rwkv_a0, rwkv_a1, rwkv_a2, rwkv_g1, rwkv_g2, rwkv_k_k, rwkv_k_a, rwkv_r_k, rwkv_ln_g, rwkv_ln_b, rwkv_w_o, norm_f):
    depth = ada_w.shape[0]
    bp, sp, d = x_prompt.shape
    bs, ls, _ = x_sample.shape
    n_heads = d // HEAD_DIM
    kv_dim = (n_heads // GQA_GROUP) * HEAD_DIM
    tm_p = _token_tile(sp)
    tm_s = _token_tile(ls)

    rows = -(-(1 + bs) // 8) * 8
    cond = jnp.zeros((rows, d), F32).at[0].set(c_ctx).at[1:1 + bs].set(c)
    mods = _ada(cond, ada_w, ada_b)

    ffn_w_in_b = ffn_w_in.astype(BF16)
    ffn_w_out_b = ffn_w_out.astype(BF16)
    rope_tables = _rope_tables(ls, GRID_W)

    xp, xs = x_prompt, x_sample
    ctx_k, ctx_v, ctx_state = [], [], []
    for l in range(depth):
        mod_p = mods[l, 0:1].reshape(1, N_ADA, 1, d)
        mod_s = mods[l, 1:1 + bs].reshape(bs, N_ADA, 1, d)
        xp = _ffn(xp, mod_p, (0, 1, 2), False, norm_g[l, 0], ffn_w_in_b[l, 0], ffn_w_out_b[l, 0], tm_p)
        xs = _ffn(xs, mod_s, (0, 1, 2), True, norm_g[l, 0], ffn_w_in_b[l, 0], ffn_w_out_b[l, 0], tm_s)
        i = l // 2
        if l % 2 == 0:
            w_qkv = attn_w_qkv[i].astype(BF16)
            w_o = attn_w_o[i].astype(BF16)
            q, k, v = _qkv(xp, mod_p, False, norm_g[l, 1], w_qkv, None, tm_p)
            ctx_k.append(k.reshape(bp, sp, kv_dim // HEAD_DIM, HEAD_DIM))
            ctx_v.append(v.reshape(bp, sp, kv_dim // HEAD_DIM, HEAD_DIM))
            xp = _attn_ctx(q, k, v, attn_sink[i], xp, mod_p, w_o)
            q, k, v = _qkv(xs, mod_s, True, norm_g[l, 1], w_qkv, rope_tables, tm_s)
            head_major = lambda t: jnp.swapaxes(t, 1, 2).astype(BF16)
            xs = _attn_lat(q, k, v, head_major(cache_k[:, i]), head_major(cache_v[:, i]),
                           attn_sink[i], xs, mod_s, w_o)
        else:
            p = dict(mu=rwkv_mu[i], w_rkv=rwkv_w_rkv[i].astype(BF16), w0=rwkv_w0[i],
                     w1=rwkv_w1[i].astype(BF16), w2=rwkv_w2[i].astype(BF16), a0=rwkv_a0[i],
                     a1=rwkv_a1[i].astype(BF16), a2=rwkv_a2[i].astype(BF16),
                     g1=rwkv_g1[i].astype(BF16), g2=rwkv_g2[i].astype(BF16),
                     k_k=rwkv_k_k[i], k_a=rwkv_k_a[i])
            w_o = rwkv_w_o[i].astype(BF16)
            r_k = rwkv_r_k[i].reshape(2, d)
            zero_state = jnp.zeros((bp, n_heads // 2, LANES, LANES), F32)
            new_states = []
            for x, mod, per_batch, tm, is_prompt in ((xp, mod_p, False, tm_p, True),
                                                     (xs, mod_s, True, tm_s, False)):
                r, v, kk, lw, kd, ag, gg = _rwkv_pre(x, mod, per_batch, norm_g[l, 1], p, tm)
                zs = []
                for dr in range(2):
                    s0 = zero_state if is_prompt else _pair_states(state_wkv[:, i, dr])
                    z, s_end = _wkv_scan(r, lw, kd, v, kk, ag, s0,
                                         rwkv_ln_g[i, dr], rwkv_ln_b[i, dr], r_k[dr], dr)
                    zs.append(z)
                    if is_prompt:
                        new_states.append(_unpair_states(s_end))
                x = _rwkv_post(zs[0], zs[1], gg, x, mod, per_batch, w_o, tm)
                if is_prompt:
                    xp = x
                else:
                    xs = x
            ctx_state.append(jnp.stack(new_states, axis=1))
        xp = _ffn(xp, mod_p, (6, 7, 8), False, norm_g[l, 2], ffn_w_in_b[l, 1], ffn_w_out_b[l, 1], tm_p)
        xs = _ffn(xs, mod_s, (6, 7, 8), True, norm_g[l, 2], ffn_w_in_b[l, 1], ffn_w_out_b[l, 1], tm_s)

    y_prompt = _final_norm(xp, norm_f, tm_p)
    y_sample = _final_norm(xs, norm_f, tm_s)
    return (y_prompt, y_sample, jnp.stack(ctx_k, axis=1), jnp.stack(ctx_v, axis=1),
            jnp.stack(ctx_state, axis=1))
```

```python
import functools

import jax
import jax.numpy as jnp
from jax import lax
from jax.experimental import pallas as pl
from jax.experimental.pallas import tpu as pltpu

F32 = jnp.float32
BF16 = jnp.bfloat16

HEAD_DIM = 64
GQA_GROUP = 4
LANES = 128
N_ADA = 9
ATTN_BLK = 128
GRID_W = 64
ROPE_BASE = 10000.0
ROPE_PAIRS = HEAD_DIM // 4
RMS_EPS = 1e-6
GN_EPS = 64e-5
KK_EPS = 1e-12
DECAY_SCALE = 0.6065306597126334
NEG_INF = -1e30
LOG2_E = 1.4426950408889634
SCAN_CHUNK = 64
SCAN_PAIRS = 8
SCAN_SUB = 4
SCAN_LAG = 3
VMEM_LIMIT = 56 * 1024 * 1024


def _params(*sem):
    return pltpu.CompilerParams(dimension_semantics=sem, vmem_limit_bytes=VMEM_LIMIT)


def _const_spec(shape):
    nd = len(shape)
    return pl.BlockSpec(shape, lambda *_: (0,) * nd, pipeline_mode=pl.Buffered(1))


def _mod_block(d_model, chunk, per_batch):
    def index(b, *_):
        return (b if per_batch else 0, chunk, 0, 0)
    return pl.BlockSpec((None, None, 1, d_model), index)


def _mod_norm(x, g, shift, scale):
    ms = jnp.mean(x * x, axis=-1, keepdims=True)
    return (x * lax.rsqrt(ms + RMS_EPS) * g) * (1.0 + scale) + shift


def _dot(a, b):
    return jnp.dot(a, b, preferred_element_type=F32)


def _dot_nt(a, b):
    return lax.dot_general(a, b, (((1,), (1,)), ((), ())), preferred_element_type=F32)


def _dot_tn(a, b):
    return lax.dot_general(a, b, (((0,), (0,)), ((), ())), preferred_element_type=F32)


def _split(a):
    hi = a.astype(BF16)
    lo = (a - hi.astype(F32)).astype(BF16)
    return hi, lo


def _dot1(a, b):
    return _dot(a.astype(BF16), b.astype(BF16))


def _dot1_nt(a, b):
    return _dot_nt(a.astype(BF16), b.astype(BF16))


def _dot2_tn(a, b):
    ah, al = _split(a)
    bh = b.astype(BF16)
    return _dot_tn(ah, bh) + _dot_tn(al, bh)


def _dot2_exact_rhs(a, b_bf16):
    ah, al = _split(a)
    return _dot(jnp.concatenate([ah, al], axis=1), jnp.concatenate([b_bf16, b_bf16], axis=0))


def _head_ones(scale):
    r = lax.broadcasted_iota(jnp.int32, (LANES, LANES), 0) // HEAD_DIM
    c = lax.broadcasted_iota(jnp.int32, (LANES, LANES), 1) // HEAD_DIM
    return jnp.where(r == c, scale, 0.0).astype(BF16)


def _ada_body(c_ref, w_ref, b_ref, o_ref):
    c = c_ref[...]
    s = (c * jax.nn.sigmoid(c)).astype(BF16)
    o_ref[...] = _dot(s, w_ref[...].astype(BF16)) + b_ref[...]


def _ada(cond, ada_w, ada_b):
    depth, d_model, n_out = ada_w.shape
    rows = cond.shape[0]
    tn = d_model
    return pl.pallas_call(
        _ada_body,
        out_shape=jax.ShapeDtypeStruct((depth, rows, n_out), F32),
        grid=(depth, n_out // tn),
        in_specs=[
            pl.BlockSpec((rows, d_model), lambda l, n: (0, 0)),
            pl.BlockSpec((None, d_model, tn), lambda l, n: (l, 0, n)),
            pl.BlockSpec((None, 1, tn), lambda l, n: (l, 0, n)),
        ],
        out_specs=pl.BlockSpec((None, rows, tn), lambda l, n: (l, 0, n)),
        compiler_params=_params("parallel", "parallel"),
        name="ada",
    )(cond, ada_w, ada_b.reshape(depth, 1, n_out))


def _ffn_body(x_ref, sh_ref, sc_ref, gt_ref, g_ref, win_ref, wout_ref, o_ref):
    x = x_ref[...]
    h = _mod_norm(x, g_ref[...], sh_ref[...], sc_ref[...]).astype(BF16)
    hh = _dot(h, win_ref[...])
    d_ff = hh.shape[1] // 2
    gate = hh[:, :d_ff]
    act = (gate * jax.nn.sigmoid(gate) * hh[:, d_ff:]).astype(BF16)
    y = _dot(act, wout_ref[...])
    o_ref[...] = x + (0.5 * gt_ref[...]) * y


def _ffn(x, mod, chunks, per_batch, g, w_in, w_out, tm):
    b, l, d = x.shape
    tok = pl.BlockSpec((None, tm, d), lambda i, t: (i, t, 0))
    return pl.pallas_call(
        _ffn_body,
        out_shape=jax.ShapeDtypeStruct(x.shape, F32),
        grid=(b, l // tm),
        in_specs=[
            tok,
            _mod_block(d, chunks[0], per_batch),
            _mod_block(d, chunks[1], per_batch),
            _mod_block(d, chunks[2], per_batch),
            _const_spec((1, d)),
            _const_spec(w_in.shape),
            _const_spec(w_out.shape),
        ],
        out_specs=tok,
        compiler_params=_params("parallel", "parallel"),
        name="ffn",
    )(x, mod, mod, mod, g.reshape(1, d), w_in, w_out)


def _rope(x, cos, sin_signed):
    width = x.shape[1]
    lane = lax.broadcasted_iota(jnp.int32, x.shape, 1)
    first_half = (lane % (2 * ROPE_PAIRS)) < ROPE_PAIRS
    partner = jnp.where(first_half,
                        pltpu.roll(x, width - ROPE_PAIRS, 1),
                        pltpu.roll(x, ROPE_PAIRS, 1))
    return x * cos + partner * sin_signed


def _qkv_body(*refs, rope, d_model, kv_dim):
    if rope:
        x_ref, sh_ref, sc_ref, g_ref, w_ref, cos_ref, sin_ref, q_ref, k_ref, v_ref = refs
    else:
        x_ref, sh_ref, sc_ref, g_ref, w_ref, q_ref, k_ref, v_ref = refs
    h = _mod_norm(x_ref[...], g_ref[...], sh_ref[...], sc_ref[...]).astype(BF16)
    qkv = _dot(h, w_ref[...])
    q = qkv[:, :d_model]
    k = qkv[:, d_model:d_model + kv_dim]
    v = qkv[:, d_model + kv_dim:]
    if rope:
        cos = cos_ref[...]
        sin = sin_ref[...]
        q = _rope(q, jnp.concatenate([cos] * (d_model // LANES), axis=1),
                  jnp.concatenate([sin] * (d_model // LANES), axis=1))
        k = _rope(k, jnp.concatenate([cos] * (kv_dim // LANES), axis=1),
                  jnp.concatenate([sin] * (kv_dim // LANES), axis=1))
    if rope:
        q = (q * (HEAD_DIM ** -0.5 * LOG2_E)).astype(BF16)
        for hd in range(d_model // HEAD_DIM):
            q_ref[hd] = q[:, hd * HEAD_DIM:(hd + 1) * HEAD_DIM]
        for hd in range(kv_dim // HEAD_DIM):
            k_ref[hd] = k[:, hd * HEAD_DIM:(hd + 1) * HEAD_DIM].astype(BF16)
            v_ref[hd] = v[:, hd * HEAD_DIM:(hd + 1) * HEAD_DIM].astype(BF16)
    else:
        q_ref[...] = (q * (HEAD_DIM ** -0.5)).astype(BF16)
        k_ref[...] = k
        v_ref[...] = v


def _qkv(x, mod, per_batch, g, w_qkv, rope_tables, tm):
    b, l, d = x.shape
    kv_dim = (w_qkv.shape[1] - d) // 2
    rope = rope_tables is not None
    tok = lambda w: pl.BlockSpec((None, tm, w), lambda i, t: (i, t, 0))
    heads = lambda w: pl.BlockSpec((None, w // HEAD_DIM, tm, HEAD_DIM), lambda i, t: (i, 0, t, 0))
    head_shape = lambda w: jax.ShapeDtypeStruct((b, w // HEAD_DIM, l, HEAD_DIM), BF16)
    in_specs = [tok(d), _mod_block(d, 3, per_batch), _mod_block(d, 4, per_batch),
                _const_spec((1, d)), _const_spec(w_qkv.shape)]
    args = [x, mod, mod, g.reshape(1, d), w_qkv]
    if rope:
        tab = pl.BlockSpec((tm, LANES), lambda i, t: (t, 0))
        in_specs += [tab, tab]
        args += list(rope_tables)
    return pl.pallas_call(
        functools.partial(_qkv_body, rope=rope, d_model=d, kv_dim=kv_dim),
        out_shape=((head_shape(d), head_shape(kv_dim), head_shape(kv_dim)) if rope else
                   (jax.ShapeDtypeStruct((b, l, d), BF16),
                    jax.ShapeDtypeStruct((b, l, kv_dim), F32),
                    jax.ShapeDtypeStruct((b, l, kv_dim), F32))),
        grid=(b, l // tm),
        in_specs=in_specs,
        out_specs=((heads(d), heads(kv_dim), heads(kv_dim)) if rope else
                   (tok(d), tok(kv_dim), tok(kv_dim))),
        compiler_params=_params("parallel", "parallel"),
        name="qkv_rope" if rope else "qkv",
    )(*args)


def _rope_tables(l, grid_w):
    pos = jnp.arange(l, dtype=jnp.int32)
    row = (pos // grid_w).astype(F32)
    col = (pos % grid_w).astype(F32)
    inv = jnp.power(ROPE_BASE, -jnp.arange(ROPE_PAIRS, dtype=F32) / ROPE_PAIRS)
    ang_r = row[:, None] * inv
    ang_c = col[:, None] * inv
    cos = jnp.concatenate([jnp.cos(ang_r)] * 2 + [jnp.cos(ang_c)] * 2, axis=1)
    sin = jnp.concatenate([-jnp.sin(ang_r), jnp.sin(ang_r),
                           -jnp.sin(ang_c), jnp.sin(ang_c)], axis=1)
    return jnp.concatenate([cos, cos], axis=1), jnp.concatenate([sin, sin], axis=1)


def _softmax_pv(scores, values, sink):
    m = jnp.maximum(functools.reduce(
        jnp.maximum, [jnp.max(s, axis=1, keepdims=True) for s in scores]), sink)
    den = jnp.exp(sink - m)
    acc = None
    for s, v in zip(scores, values):
        p = jnp.exp(s - m)
        den = den + jnp.sum(p, axis=1, keepdims=True)
        pv = _dot(p.astype(BF16), v)
        acc = pv if acc is None else acc + pv
    return acc / den


def _attn_ctx_body(sink_ref, q_ref, k_ref, v_ref, x_ref, gt_ref, wo_ref, o_ref, head_out):
    n_heads = q_ref.shape[1] // HEAD_DIM
    for kv in range(n_heads // GQA_GROUP):
        ks = slice(kv * HEAD_DIM, (kv + 1) * HEAD_DIM)
        kb = k_ref[:, ks].astype(BF16)
        vb = v_ref[:, ks].astype(BF16)
        for grp in range(GQA_GROUP):
            hd = kv * GQA_GROUP + grp
            hs = slice(hd * HEAD_DIM, (hd + 1) * HEAD_DIM)
            s = _dot_nt(q_ref[:, hs], kb)
            head_out[:, hs] = _softmax_pv([s], [vb], sink_ref[hd]).astype(BF16)
    y = _dot(head_out[...], wo_ref[...])
    o_ref[...] = x_ref[...] + gt_ref[...] * y


def _attn_ctx(q, k, v, sink, x, mod, w_o):
    b, s, d = x.shape
    kv_dim = k.shape[2]
    tok = lambda w: pl.BlockSpec((None, s, w), lambda i: (i, 0, 0))
    return pl.pallas_call(
        _attn_ctx_body,
        out_shape=jax.ShapeDtypeStruct(x.shape, F32),
        grid=(b,),
        in_specs=[pl.BlockSpec(memory_space=pltpu.SMEM),
                  tok(d), tok(kv_dim), tok(kv_dim), tok(d),
                  _mod_block(d, 5, False), _const_spec(w_o.shape)],
        out_specs=tok(d),
        scratch_shapes=[pltpu.VMEM((s, d), BF16)],
        compiler_params=_params("parallel"),
        name="attn_ctx",
    )(sink, q, k, v, x, mod, w_o)


def _attn_lat_body(sink_ref, q_ref, k_ref, v_ref, kc_ref, vc_ref, x_ref, gt_ref, wo_ref,
                   o_ref, heads_t, bias):
    n_heads, blk, _ = q_ref.shape
    seq = k_ref.shape[1]
    span = 3 * blk
    width = GQA_GROUP * blk
    j = pl.program_id(1)
    start = pl.multiple_of(jnp.clip((j - 1) * blk, 0, seq - span), blk)
    kpos = start + lax.broadcasted_iota(jnp.int32, (span, blk), 0)
    qpos = j * blk + lax.broadcasted_iota(jnp.int32, (span, blk), 1)
    bias[...] = jnp.where(jnp.abs(qpos - kpos) <= blk, 0.0, NEG_INF)
    head_of_col = lax.broadcasted_iota(jnp.int32, (1, width), 1) // blk
    n_kv = n_heads // GQA_GROUP

    def scores(kv):
        q4 = q_ref[kv * GQA_GROUP:(kv + 1) * GQA_GROUP].reshape(width, HEAD_DIM)
        kl = k_ref[kv, pl.ds(start, span), :]
        s_loc = _dot_nt(kl, q4) + jnp.concatenate([bias[...]] * GQA_GROUP, axis=1)
        return s_loc, _dot_nt(kc_ref[kv], q4)

    ahead = scores(0)
    for kv in range(n_kv):
        first = kv * GQA_GROUP
        s_loc, s_ctx = ahead
        if kv + 1 < n_kv:
            ahead = scores(kv + 1)
        vl = v_ref[kv, pl.ds(start, span), :]
        sink = jnp.zeros((1, width), F32)
        for grp in range(GQA_GROUP):
            sink = jnp.where(head_of_col == grp, sink_ref[first + grp] * LOG2_E, sink)
        m = jnp.maximum(jnp.maximum(jnp.max(s_loc, axis=0, keepdims=True),
                                    jnp.max(s_ctx, axis=0, keepdims=True)), sink)
        p_loc = jnp.exp2(s_loc - m)
        p_ctx = jnp.exp2(s_ctx - m)
        den = (jnp.sum(p_loc, axis=0, keepdims=True) + jnp.sum(p_ctx, axis=0, keepdims=True)
               + jnp.exp2(sink - m))
        out_t = (_dot_tn(vl, p_loc.astype(BF16)) + _dot_tn(vc_ref[kv], p_ctx.astype(BF16))) / den
        for grp in range(GQA_GROUP):
            hd = first + grp
            heads_t[hd * HEAD_DIM:(hd + 1) * HEAD_DIM, :] = (
                out_t[:, grp * blk:(grp + 1) * blk].astype(BF16))
    y = _dot_tn(heads_t[...], wo_ref[...])
    o_ref[...] = x_ref[...] + gt_ref[...] * y


def _attn_lat(q, k, v, kc, vc, sink, x, mod, w_o):
    b, l, d = x.shape
    n_heads, n_kv, past = q.shape[1], k.shape[1], kc.shape[2]
    assert l % ATTN_BLK == 0 and l >= 3 * ATTN_BLK
    blk = pl.BlockSpec((None, ATTN_BLK, d), lambda i, j: (i, j, 0))
    full = lambda n: pl.BlockSpec((None, n_kv, n, HEAD_DIM), lambda i, j: (i, 0, 0, 0))
    return pl.pallas_call(
        _attn_lat_body,
        out_shape=jax.ShapeDtypeStruct(x.shape, F32),
        grid=(b, l // ATTN_BLK),
        in_specs=[pl.BlockSpec(memory_space=pltpu.SMEM),
                  pl.BlockSpec((None, n_heads, ATTN_BLK, HEAD_DIM), lambda i, j: (i, 0, j, 0)),
                  full(l), full(l), full(past), full(past), blk,
                  _mod_block(d, 5, True), _const_spec(w_o.shape)],
        out_specs=blk,
        scratch_shapes=[pltpu.VMEM((d, ATTN_BLK), BF16),
                        pltpu.VMEM((3 * ATTN_BLK, ATTN_BLK), F32)],
        compiler_params=_params("parallel", "arbitrary"),
        name="attn_lat",
    )(sink, q, k, v, kc, vc, x, mod, w_o)


def _segment_apply(z, mat_bf16):
    return jnp.concatenate(
        [_dot2_exact_rhs(z[:, p * LANES:(p + 1) * LANES], mat_bf16)
         for p in range(z.shape[1] // LANES)], axis=1)


def _rwkv_pre_body(x_ref, xp_ref, xn_ref, sh_ref, sc_ref, g_ref, mu_ref, wrkv_ref,
                   w0_ref, w1_ref, w2_ref, a0_ref, a1_ref, a2_ref, g1_ref, g2_ref,
                   kk_scale_ref, ka_ref,
                   r_ref, v_ref, kk_ref, lw_ref, kd_ref, a_ref, gg_ref):
    t = pl.program_id(1)
    nt = pl.num_programs(1)
    g = g_ref[...]
    sh = sh_ref[...]
    sc = sc_ref[...]
    h = _mod_norm(x_ref[...], g, sh, sc)
    tm = h.shape[0]
    halo = xp_ref.shape[0]
    h_before = _mod_norm(xp_ref[...], g, sh, sc)[halo - 1:halo]
    h_after = _mod_norm(xn_ref[...], g, sh, sc)[0:1]
    h_before = jnp.where(t == 0, 0.0, h_before)
    h_after = jnp.where(t == nt - 1, 0.0, h_after)
    row = lax.broadcasted_iota(jnp.int32, h.shape, 0)
    prev = jnp.where(row == 0, h_before, pltpu.roll(h, 1, 0))
    nxt = jnp.where(row == tm - 1, h_after, pltpu.roll(h, tm - 1, 0))
    xx = 0.5 * (prev + nxt) - h
    mix = lambda i: (h + xx * mu_ref[i:i + 1, :]).astype(BF16)
    r = _dot(mix(0), wrkv_ref[0])
    xw = mix(1)
    k = _dot(mix(2), wrkv_ref[1])
    v = _dot(mix(3), wrkv_ref[2])
    xa = mix(4)
    xg = mix(5)
    r_ref[...] = r
    v_ref[...] = v
    kk = k * kk_scale_ref[...]
    ssq = _segment_apply(kk * kk, _head_ones(1.0))
    kk_ref[...] = kk * lax.rsqrt(ssq + KK_EPS)
    ka = ka_ref[...]
    for d in range(2):
        lora = _dot(jnp.tanh(_dot(xw, w1_ref[d])).astype(BF16), w2_ref[d])
        lw_ref[d] = -DECAY_SCALE * jax.nn.sigmoid(w0_ref[d:d + 1, :] + lora)
        a = jax.nn.sigmoid(a0_ref[d:d + 1, :] + _dot(_dot(xa, a1_ref[d]).astype(BF16), a2_ref[d]))
        a_ref[d] = a
        kd_ref[d] = k * (1.0 + (a - 1.0) * ka)
        gg_ref[d] = _dot(jax.nn.sigmoid(_dot(xg, g1_ref[d])).astype(BF16), g2_ref[d])


def _rwkv_pre(x, mod, per_batch, g, p, tm):
    b, l, d = x.shape
    halo = 8
    nh = l // halo
    tok = pl.BlockSpec((None, tm, d), lambda i, t: (i, t, 0))
    tok2 = pl.BlockSpec((2, None, tm, d), lambda i, t: (0, i, t, 0))
    before = pl.BlockSpec((None, halo, d),
                          lambda i, t: (i, jnp.maximum(t * (tm // halo) - 1, 0), 0))
    after = pl.BlockSpec((None, halo, d),
                         lambda i, t: (i, jnp.minimum((t + 1) * (tm // halo), nh - 1), 0))
    one = jax.ShapeDtypeStruct((b, l, d), F32)
    two = jax.ShapeDtypeStruct((2, b, l, d), F32)
    consts = [g.reshape(1, d), p["mu"], p["w_rkv"], p["w0"], p["w1"], p["w2"], p["a0"],
              p["a1"], p["a2"], p["g1"], p["g2"], p["k_k"].reshape(1, d),
              p["k_a"].reshape(1, d)]
    return pl.pallas_call(
        _rwkv_pre_body,
        out_shape=(one, one, one, two, two, two, two),
        grid=(b, l // tm),
        in_specs=[tok, before, after, _mod_block(d, 3, per_batch), _mod_block(d, 4, per_batch)]
                 + [_const_spec(c.shape) for c in consts],
        out_specs=(tok, tok, tok, tok2, tok2, tok2, tok2),
        compiler_params=_params("parallel", "parallel"),
        name="rwkv_pre",
    )(x, x, x, mod, mod, *consts)


def _stack_heads(x, first_head):
    return jnp.concatenate([jnp.where(first_head, x, 0.0), jnp.where(first_head, 0.0, x)],
                           axis=0)


def _wkv_masks(chunk, reverse):
    rows = 2 * chunk
    ti = lax.broadcasted_iota(jnp.int32, (chunk, chunk), 0)
    si = lax.broadcasted_iota(jnp.int32, (chunk, chunk), 1)
    tri = jnp.where((si >= ti) if reverse else (si <= ti), 1.0, 0.0).astype(BF16)
    tt = lax.broadcasted_iota(jnp.int32, (rows, rows), 0)
    ss = lax.broadcasted_iota(jnp.int32, (rows, rows), 1)
    return dict(
        tri3=jnp.concatenate([tri, tri, tri], axis=1),
        first_head=lax.broadcasted_iota(jnp.int32, (chunk, LANES), 1) < HEAD_DIM,
        strict=(ss % chunk > tt % chunk) if reverse else (ss % chunk < tt % chunk),
        incl=(ss % chunk >= tt % chunk) if reverse else (ss % chunk <= tt % chunk),
        eye=jnp.where(tt == ss, 1.0, 0.0),
        avg=_head_ones(1.0 / HEAD_DIM),
        ones=_head_ones(1.0))


def _wkv_chain(r, lw, k, v, kk, ag, get_state, put_state, ln_g, ln_b, r_k, reverse, c):
    chunk = r.shape[0]
    rows = 2 * chunk
    cat = lambda *parts: jnp.concatenate(parts, axis=0)
    stack = lambda x: _stack_heads(x, c["first_head"])

    hi = lw.astype(BF16)
    mid = (lw - hi.astype(F32)).astype(BF16)
    lo = (lw - hi.astype(F32) - mid.astype(F32)).astype(BF16)
    cum = _dot(c["tri3"], cat(hi, mid, lo))
    yield
    total = cum[0:1] if reverse else cum[chunk - 1:chunk]
    b_vec = kk * ag
    grow = jnp.exp(-cum)
    to_end = jnp.exp(total - cum)
    lhs = cat(stack(-kk * jnp.exp(cum - lw)), stack(r * jnp.exp(cum)))
    rhs = cat(stack(b_vec * grow), stack(k * grow))
    rhs_end = cat(stack(b_vec * to_end), stack(k * to_end))
    vb = stack(v)
    gram = _dot1_nt(lhs, rhs)
    yield
    l_ab = jnp.where(c["strict"], gram[:rows, :rows], 0.0)
    l_ak = jnp.where(c["strict"], gram[:rows, rows:], 0.0)
    m_rb = jnp.where(c["incl"], gram[rows:, :rows], 0.0)
    m_rk = jnp.where(c["incl"], gram[rows:, rows:], 0.0)
    inv = c["eye"] + l_ab
    power = _dot1(l_ab, l_ab)
    yield
    span = 2
    while 2 * span < chunk:
        both = _dot1(power, jnp.concatenate([inv, power], axis=1))
        yield
        inv = inv + both[:, :rows]
        power = both[:, rows:]
        span *= 2
    inv = inv + _dot1(power, inv)
    yield
    s_prev = get_state()
    drive = _dot1(jnp.concatenate([lhs, cat(l_ak, m_rk)], axis=1), cat(s_prev, vb))
    yield
    u = _dot1(inv, drive[:rows])
    yield
    y_stacked = drive[rows:] + _dot1(m_rb, u)
    decay_rows = jnp.transpose(jnp.broadcast_to(jnp.exp(total), (LANES, LANES)))
    put_state(s_prev * decay_rows + _dot2_tn(rhs_end, cat(u, vb)))
    yield
    y = y_stacked[:chunk] + y_stacked[chunk:]
    cen = y - _dot2_exact_rhs(y, c["avg"])
    var = _dot2_exact_rhs(cen * cen, c["avg"])
    bonus = _dot2_exact_rhs(r * k * r_k, c["ones"]) * v
    yield
    return cen * lax.rsqrt(var + GN_EPS) * ln_g + ln_b + bonus


def _wkv_body(r_ref, lw_ref, k_ref, v_ref, kk_ref, ag_ref, s0_ref, lng_ref, lnb_ref, rk_ref,
              z_ref, sout_ref, state, *, reverse):
    c = pl.program_id(2)
    nc = pl.num_programs(2)
    group = state.shape[0]
    n_sub = r_ref.shape[0] // SCAN_CHUNK

    @pl.when(c == 0)
    def _():
        state[...] = s0_ref[...]

    consts = _wkv_masks(SCAN_CHUNK, reverse)
    order = list(range(n_sub))[::-1] if reverse else list(range(n_sub))
    carried = [[None] * (n_sub + 1) for _ in range(group)]
    chains, starts, where = [], [], []
    for step, sub in enumerate(order):
        rows = slice(sub * SCAN_CHUNK, (sub + 1) * SCAN_CHUNK)
        for i in range(group):
            ls = slice(i * LANES, (i + 1) * LANES)
            get = ((lambda i=i: state[i]) if step == 0 else
                   (lambda i=i, step=step: carried[i][step]))
            put = lambda s, i=i, step=step: carried[i].__setitem__(step + 1, s)
            chains.append(_wkv_chain(r_ref[rows, ls], lw_ref[rows, ls], k_ref[rows, ls],
                                     v_ref[rows, ls], kk_ref[rows, ls], ag_ref[rows, ls], get, put,
                                     lng_ref[:, ls], lnb_ref[:, ls], rk_ref[:, ls], reverse, consts))
            starts.append(step * SCAN_LAG)
            where.append((rows, ls))
    done = [None] * len(chains)
    tick = 0
    while any(d is None for d in done):
        for n, chain in enumerate(chains):
            if done[n] is None and tick >= starts[n]:
                try:
                    next(chain)
                except StopIteration as stop:
                    done[n] = stop.value
        tick += 1
    for n, (rows, ls) in enumerate(where):
        z_ref[rows, ls] = done[n]
    for i in range(group):
        state[i] = carried[i][n_sub]

    @pl.when(c == nc - 1)
    def _():
        for i in range(group):
            sout_ref[i] = carried[i][n_sub]


def _wkv_scan(r, lw, kd, v, kk, ag, s0, ln_g, ln_b, r_k, direction):
    b, l, d = r.shape
    reverse = direction == 1
    group = min(SCAN_PAIRS, d // LANES)
    width = group * LANES
    step_rows = SCAN_SUB * SCAN_CHUNK
    nc = l // step_rows
    at = (lambda c: nc - 1 - c) if reverse else (lambda c: c)
    tok = pl.BlockSpec((None, step_rows, width), lambda i, p, c: (i, at(c), p))
    tok2 = pl.BlockSpec((None, None, step_rows, width), lambda i, p, c: (direction, i, at(c), p))
    st = pl.BlockSpec((None, group, LANES, LANES), lambda i, p, c: (i, p, 0, 0))
    vec = pl.BlockSpec((1, width), lambda i, p, c: (0, p))
    return pl.pallas_call(
        functools.partial(_wkv_body, reverse=reverse),
        out_shape=(jax.ShapeDtypeStruct((b, l, d), F32),
                   jax.ShapeDtypeStruct(s0.shape, F32)),
        grid=(b, d // width, nc),
        in_specs=[tok, tok2, tok2, tok, tok, tok2, st, vec, vec, vec],
        out_specs=(tok, st),
        scratch_shapes=[pltpu.VMEM((group, LANES, LANES), F32)],
        compiler_params=_params("parallel", "parallel", "arbitrary"),
        name="wkv_rev" if reverse else "wkv_fwd",
    )(r, lw, kd, v, kk, ag, s0, ln_g.reshape(1, d), ln_b.reshape(1, d), r_k.reshape(1, d))


def _pair_states(s):
    b, h, n, _ = s.shape
    s = s.reshape(b, h // 2, 2, n, n)
    z = jnp.zeros_like(s[:, :, 0])
    top = jnp.concatenate([s[:, :, 0], z], axis=-1)
    bot = jnp.concatenate([z, s[:, :, 1]], axis=-1)
    return jnp.concatenate([top, bot], axis=-2)


def _unpair_states(s):
    b, p, _, _ = s.shape
    n = HEAD_DIM
    return jnp.stack([s[:, :, :n, :n], s[:, :, n:, n:]], axis=2).reshape(b, 2 * p, n, n)


def _rwkv_post_body(zf_ref, zb_ref, gg_ref, x_ref, gt_ref, wo_ref, o_ref):
    y = (zf_ref[...] * gg_ref[0] + zb_ref[...] * gg_ref[1]).astype(BF16)
    o_ref[...] = x_ref[...] + gt_ref[...] * _dot(y, wo_ref[...])


def _rwkv_post(zf, zb, gg, x, mod, per_batch, w_o, tm):
    b, l, d = x.shape
    tok = pl.BlockSpec((None, tm, d), lambda i, t: (i, t, 0))
    tok2 = pl.BlockSpec((2, None, tm, d), lambda i, t: (0, i, t, 0))
    return pl.pallas_call(
        _rwkv_post_body,
        out_shape=jax.ShapeDtypeStruct(x.shape, F32),
        grid=(b, l // tm),
        in_specs=[tok, tok, tok2, tok, _mod_block(d, 5, per_batch), _const_spec(w_o.shape)],
        out_specs=tok,
        compiler_params=_params("parallel", "parallel"),
        name="rwkv_post",
    )(zf, zb, gg, x, mod, w_o)


def _final_norm_body(x_ref, g_ref, o_ref):
    x = x_ref[...]
    ms = jnp.mean(x * x, axis=-1, keepdims=True)
    o_ref[...] = x * lax.rsqrt(ms + RMS_EPS) * g_ref[...]


def _final_norm(x, g, tm):
    b, l, d = x.shape
    tok = pl.BlockSpec((None, tm, d), lambda i, t: (i, t, 0))
    return pl.pallas_call(
        _final_norm_body,
        out_shape=jax.ShapeDtypeStruct(x.shape, F32),
        grid=(b, l // tm),
        in_specs=[tok, _const_spec((1, d))],
        out_specs=tok,
        compiler_params=_params("parallel", "parallel"),
        name="final_norm",
    )(x, g.reshape(1, d))


def _token_tile(l):
    return 256 if l % 256 == 0 else 128


def kernel(x_prompt, x_sample, cache_k, cache_v, state_wkv, c, c_ctx, ada_w, ada_b, norm_g, ffn_w_in, ffn_w_out, attn_w_qkv, attn_w_o, attn_sink, rwkv_mu, rwkv_w_rkv, rwkv_w0, rwkv_w1, rwkv_w2, rwkv_a0, rwkv_a1, rwkv_a2, rwkv_g1, rwkv_g2, rwkv_k_k, rwkv_k_a, rwkv_r_k, rwkv_ln_g, rwkv_ln_b, rwkv_w_o, norm_f):
    depth = ada_w.shape[0]
    bp, sp, d = x_prompt.shape
    bs, ls, _ = x_sample.shape
    n_heads = d // HEAD_DIM
    kv_dim = (n_heads // GQA_GROUP) * HEAD_DIM
    tm_p = _token_tile(sp)
    tm_s = _token_tile(ls)

    rows = -(-(1 + bs) // 8) * 8
    cond = jnp.zeros((rows, d), F32).at[0].set(c_ctx).at[1:1 + bs].set(c)
    mods = _ada(cond, ada_w, ada_b)

    ffn_w_in_b = ffn_w_in.astype(BF16)
    ffn_w_out_b = ffn_w_out.astype(BF16)
    rope_tables = _rope_tables(ls, GRID_W)

    xp, xs = x_prompt, x_sample
    ctx_k, ctx_v, ctx_state = [], [], []
    for l in range(depth):
        mod_p = mods[l, 0:1].reshape(1, N_ADA, 1, d)
        mod_s = mods[l, 1:1 + bs].reshape(bs, N_ADA, 1, d)
        xp = _ffn(xp, mod_p, (0, 1, 2), False, norm_g[l, 0], ffn_w_in_b[l, 0], ffn_w_out_b[l, 0], tm_p)
        xs = _ffn(xs, mod_s, (0, 1, 2), True, norm_g[l, 0], ffn_w_in_b[l, 0], ffn_w_out_b[l, 0], tm_s)
        i = l // 2
        if l % 2 == 0:
            w_qkv = attn_w_qkv[i].astype(BF16)
            w_o = attn_w_o[i].astype(BF16)
            q, k, v = _qkv(xp, mod_p, False, norm_g[l, 1], w_qkv, None, tm_p)
            ctx_k.append(k.reshape(bp, sp, kv_dim // HEAD_DIM, HEAD_DIM))
            ctx_v.append(v.reshape(bp, sp, kv_dim // HEAD_DIM, HEAD_DIM))
            xp = _attn_ctx(q, k, v, attn_sink[i], xp, mod_p, w_o)
            q, k, v = _qkv(xs, mod_s, True, norm_g[l, 1], w_qkv, rope_tables, tm_s)
            head_major = lambda t: jnp.swapaxes(t, 1, 2).astype(BF16)
            xs = _attn_lat(q, k, v, head_major(cache_k[:, i]), head_major(cache_v[:, i]),
                           attn_sink[i], xs, mod_s, w_o)
        else:
            p = dict(mu=rwkv_mu[i], w_rkv=rwkv_w_rkv[i].astype(BF16), w0=rwkv_w0[i],
                     w1=rwkv_w1[i].astype(BF16), w2=rwkv_w2[i].astype(BF16), a0=rwkv_a0[i],
                     a1=rwkv_a1[i].astype(BF16), a2=rwkv_a2[i].astype(BF16),
                     g1=rwkv_g1[i].astype(BF16), g2=rwkv_g2[i].astype(BF16),
                     k_k=rwkv_k_k[i], k_a=rwkv_k_a[i])
            w_o = rwkv_w_o[i].astype(BF16)
            r_k = rwkv_r_k[i].reshape(2, d)
            zero_state = jnp.zeros((bp, n_heads // 2, LANES, LANES), F32)
            new_states = []
            for x, mod, per_batch, tm, is_prompt in ((xp, mod_p, False, tm_p, True),
                                                     (xs, mod_s, True, tm_s, False)):
                r, v, kk, lw, kd, ag, gg = _rwkv_pre(x, mod, per_batch, norm_g[l, 1], p, tm)
                zs = []
                for dr in range(2):
                    s0 = (zero_state if is_prompt else
                          _pair_states(jnp.swapaxes(state_wkv[:, i, dr], -1, -2)))
                    z, s_end = _wkv_scan(r, lw, kd, v, kk, ag, s0,
                                         rwkv_ln_g[i, dr], rwkv_ln_b[i, dr], r_k[dr], dr)
                    zs.append(z)
                    if is_prompt:
                        new_states.append(jnp.swapaxes(_unpair_states(s_end), -1, -2))
                x = _rwkv_post(zs[0], zs[1], gg, x, mod, per_batch, w_o, tm)
                if is_prompt:
                    xp = x
                else:
                    xs = x
            ctx_state.append(jnp.stack(new_states, axis=1))
        xp = _ffn(xp, mod_p, (6, 7, 8), False, norm_g[l, 2], ffn_w_in_b[l, 1], ffn_w_out_b[l, 1], tm_p)
        xs = _ffn(xs, mod_s, (6, 7, 8), True, norm_g[l, 2], ffn_w_in_b[l, 1], ffn_w_out_b[l, 1], tm_s)

    y_prompt = _final_norm(xp, norm_f, tm_p)
    y_sample = _final_norm(xs, norm_f, tm_s)
    return (y_prompt, y_sample, jnp.stack(ctx_k, axis=1), jnp.stack(ctx_v, axis=1),
            jnp.stack(ctx_state, axis=1))
```

```python
import functools

import jax
import jax.numpy as jnp
from jax import lax
from jax.experimental import pallas as pl
from jax.experimental.pallas import tpu as pltpu

F32 = jnp.float32
BF16 = jnp.bfloat16

HEAD_DIM = 64
GQA_GROUP = 4
LANES = 128
N_ADA = 9
ATTN_BLK = 128
GRID_W = 64
ROPE_BASE = 10000.0
ROPE_PAIRS = HEAD_DIM // 4
RMS_EPS = 1e-6
GN_EPS = 64e-5
KK_EPS = 1e-12
DECAY_SCALE = 0.6065306597126334
NEG_INF = -1e30
LOG2_E = 1.4426950408889634
FFN_TILE = 512
SCAN_CHUNK = 64
SCAN_PAIRS = 8
SCAN_SUB = 4
SCAN_LAG = 3
VMEM_LIMIT = 56 * 1024 * 1024


def _params(*sem):
    return pltpu.CompilerParams(dimension_semantics=sem, vmem_limit_bytes=VMEM_LIMIT)


def _const_spec(shape):
    nd = len(shape)
    return pl.BlockSpec(shape, lambda *_: (0,) * nd, pipeline_mode=pl.Buffered(1))


def _mod_block(d_model, chunk, per_batch):
    def index(b, *_):
        return (b if per_batch else 0, chunk, 0, 0)
    return pl.BlockSpec((None, None, 1, d_model), index)


def _mod_norm(x, g, shift, scale):
    ms = jnp.mean(x * x, axis=-1, keepdims=True)
    return (x * lax.rsqrt(ms + RMS_EPS) * g) * (1.0 + scale) + shift


def _sigmoid(x):
    return 0.5 * jnp.tanh(0.5 * x) + 0.5


def _dot(a, b):
    return jnp.dot(a, b, preferred_element_type=F32)


def _dot_nt(a, b):
    return lax.dot_general(a, b, (((1,), (1,)), ((), ())), preferred_element_type=F32)


def _dot_tn(a, b):
    return lax.dot_general(a, b, (((0,), (0,)), ((), ())), preferred_element_type=F32)


def _split(a):
    hi = a.astype(BF16)
    lo = (a - hi.astype(F32)).astype(BF16)
    return hi, lo


def _dot1(a, b):
    return _dot(a.astype(BF16), b.astype(BF16))


def _dot1_nt(a, b):
    return _dot_nt(a.astype(BF16), b.astype(BF16))


def _dot2_tn(a, b):
    ah, al = _split(a)
    bh = b.astype(BF16)
    return _dot_tn(ah, bh) + _dot_tn(al, bh)


def _dot2_exact_rhs(a, b_bf16):
    ah, al = _split(a)
    return _dot(jnp.concatenate([ah, al], axis=1), jnp.concatenate([b_bf16, b_bf16], axis=0))


def _head_ones(scale):
    r = lax.broadcasted_iota(jnp.int32, (LANES, LANES), 0) // HEAD_DIM
    c = lax.broadcasted_iota(jnp.int32, (LANES, LANES), 1) // HEAD_DIM
    return jnp.where(r == c, scale, 0.0).astype(BF16)


def _ada_body(c_ref, w_ref, b_ref, o_ref):
    c = c_ref[...]
    s = (c * jax.nn.sigmoid(c)).astype(BF16)
    o_ref[...] = _dot(s, w_ref[...].astype(BF16)) + b_ref[...]


def _ada(cond, ada_w, ada_b):
    depth, d_model, n_out = ada_w.shape
    rows = cond.shape[0]
    tn = d_model
    return pl.pallas_call(
        _ada_body,
        out_shape=jax.ShapeDtypeStruct((depth, rows, n_out), F32),
        grid=(depth, n_out // tn),
        in_specs=[
            pl.BlockSpec((rows, d_model), lambda l, n: (0, 0)),
            pl.BlockSpec((None, d_model, tn), lambda l, n: (l, 0, n)),
            pl.BlockSpec((None, 1, tn), lambda l, n: (l, 0, n)),
        ],
        out_specs=pl.BlockSpec((None, rows, tn), lambda l, n: (l, 0, n)),
        compiler_params=_params("parallel", "parallel"),
        name="ada",
    )(cond, ada_w, ada_b.reshape(depth, 1, n_out))


def _ffn_body(x_ref, sh_ref, sc_ref, gt_ref, g_ref, win_ref, wout_ref, *rest):
    final_g_ref, o_ref = rest if len(rest) == 2 else (None, rest[0])
    x = x_ref[...]
    h = _mod_norm(x, g_ref[...], sh_ref[...], sc_ref[...]).astype(BF16)
    hh = _dot(h, win_ref[...])
    d_ff = hh.shape[1] // 2
    gate = hh[:, :d_ff]
    act = (gate * jax.nn.sigmoid(gate) * hh[:, d_ff:]).astype(BF16)
    y = _dot(act, wout_ref[...])
    out = x + (0.5 * gt_ref[...]) * y
    if final_g_ref is not None:
        ms = jnp.mean(out * out, axis=-1, keepdims=True)
        out = out * lax.rsqrt(ms + RMS_EPS) * final_g_ref[...]
    o_ref[...] = out


def _ffn(x, mod, chunks, per_batch, g, w_in, w_out, final_g=None):
    shape = x.shape
    d = shape[-1]
    if not per_batch:
        x = x.reshape(1, -1, d)
    b, l, _ = x.shape
    tm = FFN_TILE if l % FFN_TILE == 0 else _token_tile(l)
    tok = pl.BlockSpec((None, tm, d), lambda i, t: (i, t, 0))
    extra = [] if final_g is None else [final_g.reshape(1, d)]
    return pl.pallas_call(
        _ffn_body,
        out_shape=jax.ShapeDtypeStruct(x.shape, F32),
        grid=(b, l // tm),
        in_specs=[
            tok,
            _mod_block(d, chunks[0], per_batch),
            _mod_block(d, chunks[1], per_batch),
            _mod_block(d, chunks[2], per_batch),
            _const_spec((1, d)),
            _const_spec(w_in.shape),
            _const_spec(w_out.shape),
        ] + [_const_spec((1, d))] * len(extra),
        out_specs=tok,
        compiler_params=_params("parallel", "parallel"),
        name="ffn",
    )(x, mod, mod, mod, g.reshape(1, d), w_in, w_out, *extra).reshape(shape)


def _rope(x, cos, sin_signed):
    width = x.shape[1]
    lane = lax.broadcasted_iota(jnp.int32, x.shape, 1)
    first_half = (lane % (2 * ROPE_PAIRS)) < ROPE_PAIRS
    partner = jnp.where(first_half,
                        pltpu.roll(x, width - ROPE_PAIRS, 1),
                        pltpu.roll(x, ROPE_PAIRS, 1))
    return x * cos + partner * sin_signed


def _qkv_body(*refs, rope, d_model, kv_dim):
    if rope:
        x_ref, sh_ref, sc_ref, g_ref, w_ref, cos_ref, sin_ref, q_ref, k_ref, v_ref = refs
    else:
        x_ref, sh_ref, sc_ref, g_ref, w_ref, q_ref, k_ref, v_ref = refs
    h = _mod_norm(x_ref[...], g_ref[...], sh_ref[...], sc_ref[...]).astype(BF16)
    qkv = _dot(h, w_ref[...])
    q = qkv[:, :d_model]
    k = qkv[:, d_model:d_model + kv_dim]
    v = qkv[:, d_model + kv_dim:]
    if rope:
        cos = cos_ref[...]
        sin = sin_ref[...]
        q = _rope(q, jnp.concatenate([cos] * (d_model // LANES), axis=1),
                  jnp.concatenate([sin] * (d_model // LANES), axis=1))
        k = _rope(k, jnp.concatenate([cos] * (kv_dim // LANES), axis=1),
                  jnp.concatenate([sin] * (kv_dim // LANES), axis=1))
    if rope:
        q = (q * (HEAD_DIM ** -0.5 * LOG2_E)).astype(BF16)
        for hd in range(d_model // HEAD_DIM):
            q_ref[hd] = q[:, hd * HEAD_DIM:(hd + 1) * HEAD_DIM]
        for hd in range(kv_dim // HEAD_DIM):
            k_ref[hd] = k[:, hd * HEAD_DIM:(hd + 1) * HEAD_DIM].astype(BF16)
            v_ref[hd] = v[:, hd * HEAD_DIM:(hd + 1) * HEAD_DIM].astype(BF16)
    else:
        q_ref[...] = (q * (HEAD_DIM ** -0.5)).astype(BF16)
        k_ref[...] = k
        v_ref[...] = v


def _qkv(x, mod, per_batch, g, w_qkv, rope_tables, tm):
    b, l, d = x.shape
    kv_dim = (w_qkv.shape[1] - d) // 2
    rope = rope_tables is not None
    tok = lambda w: pl.BlockSpec((None, tm, w), lambda i, t: (i, t, 0))
    heads = lambda w: pl.BlockSpec((None, w // HEAD_DIM, tm, HEAD_DIM), lambda i, t: (i, 0, t, 0))
    head_shape = lambda w: jax.ShapeDtypeStruct((b, w // HEAD_DIM, l, HEAD_DIM), BF16)
    in_specs = [tok(d), _mod_block(d, 3, per_batch), _mod_block(d, 4, per_batch),
                _const_spec((1, d)), _const_spec(w_qkv.shape)]
    args = [x, mod, mod, g.reshape(1, d), w_qkv]
    if rope:
        tab = pl.BlockSpec((tm, LANES), lambda i, t: (t, 0))
        in_specs += [tab, tab]
        args += list(rope_tables)
    return pl.pallas_call(
        functools.partial(_qkv_body, rope=rope, d_model=d, kv_dim=kv_dim),
        out_shape=((head_shape(d), head_shape(kv_dim), head_shape(kv_dim)) if rope else
                   (jax.ShapeDtypeStruct((b, l, d), BF16),
                    jax.ShapeDtypeStruct((b, l, kv_dim), F32),
                    jax.ShapeDtypeStruct((b, l, kv_dim), F32))),
        grid=(b, l // tm),
        in_specs=in_specs,
        out_specs=((heads(d), heads(kv_dim), heads(kv_dim)) if rope else
                   (tok(d), tok(kv_dim), tok(kv_dim))),
        compiler_params=_params("parallel", "parallel"),
        name="qkv_rope" if rope else "qkv",
    )(*args)


def _rope_tables(l, grid_w):
    pos = jnp.arange(l, dtype=jnp.int32)
    row = (pos // grid_w).astype(F32)
    col = (pos % grid_w).astype(F32)
    inv = jnp.power(ROPE_BASE, -jnp.arange(ROPE_PAIRS, dtype=F32) / ROPE_PAIRS)
    ang_r = row[:, None] * inv
    ang_c = col[:, None] * inv
    cos = jnp.concatenate([jnp.cos(ang_r)] * 2 + [jnp.cos(ang_c)] * 2, axis=1)
    sin = jnp.concatenate([-jnp.sin(ang_r), jnp.sin(ang_r),
                           -jnp.sin(ang_c), jnp.sin(ang_c)], axis=1)
    return jnp.concatenate([cos, cos], axis=1), jnp.concatenate([sin, sin], axis=1)


def _softmax_pv(scores, values, sink):
    m = jnp.maximum(functools.reduce(
        jnp.maximum, [jnp.max(s, axis=1, keepdims=True) for s in scores]), sink)
    den = jnp.exp(sink - m)
    acc = None
    for s, v in zip(scores, values):
        p = jnp.exp(s - m)
        den = den + jnp.sum(p, axis=1, keepdims=True)
        pv = _dot(p.astype(BF16), v)
        acc = pv if acc is None else acc + pv
    return acc / den


def _attn_ctx_body(sink_ref, q_ref, k_ref, v_ref, x_ref, gt_ref, wo_ref, o_ref, head_out):
    n_heads = q_ref.shape[1] // HEAD_DIM
    for kv in range(n_heads // GQA_GROUP):
        ks = slice(kv * HEAD_DIM, (kv + 1) * HEAD_DIM)
        kb = k_ref[:, ks].astype(BF16)
        vb = v_ref[:, ks].astype(BF16)
        for grp in range(GQA_GROUP):
            hd = kv * GQA_GROUP + grp
            hs = slice(hd * HEAD_DIM, (hd + 1) * HEAD_DIM)
            s = _dot_nt(q_ref[:, hs], kb)
            head_out[:, hs] = _softmax_pv([s], [vb], sink_ref[hd]).astype(BF16)
    y = _dot(head_out[...], wo_ref[...])
    o_ref[...] = x_ref[...] + gt_ref[...] * y


def _attn_ctx(q, k, v, sink, x, mod, w_o):
    b, s, d = x.shape
    kv_dim = k.shape[2]
    tok = lambda w: pl.BlockSpec((None, s, w), lambda i: (i, 0, 0))
    return pl.pallas_call(
        _attn_ctx_body,
        out_shape=jax.ShapeDtypeStruct(x.shape, F32),
        grid=(b,),
        in_specs=[pl.BlockSpec(memory_space=pltpu.SMEM),
                  tok(d), tok(kv_dim), tok(kv_dim), tok(d),
                  _mod_block(d, 5, False), _const_spec(w_o.shape)],
        out_specs=tok(d),
        scratch_shapes=[pltpu.VMEM((s, d), BF16)],
        compiler_params=_params("parallel"),
        name="attn_ctx",
    )(sink, q, k, v, x, mod, w_o)


def _attn_lat_body(sink_ref, q_ref, k_ref, v_ref, kc_ref, vc_ref, x_ref, gt_ref, wo_ref,
                   o_ref, heads_t, bias):
    n_heads, blk, _ = q_ref.shape
    seq = k_ref.shape[1]
    span = 3 * blk
    width = GQA_GROUP * blk
    j = pl.program_id(1)
    start = pl.multiple_of(jnp.clip((j - 1) * blk, 0, seq - span), blk)
    kpos = start + lax.broadcasted_iota(jnp.int32, (span, blk), 0)
    qpos = j * blk + lax.broadcasted_iota(jnp.int32, (span, blk), 1)
    bias[...] = jnp.where(jnp.abs(qpos - kpos) <= blk, 0.0, NEG_INF)
    head_of_col = lax.broadcasted_iota(jnp.int32, (1, width), 1) // blk
    n_kv = n_heads // GQA_GROUP

    def scores(kv):
        q4 = q_ref[kv * GQA_GROUP:(kv + 1) * GQA_GROUP].reshape(width, HEAD_DIM)
        kl = k_ref[kv, pl.ds(start, span), :]
        s_loc = _dot_nt(kl, q4) + jnp.concatenate([bias[...]] * GQA_GROUP, axis=1)
        return s_loc, _dot_nt(kc_ref[kv], q4)

    ahead = scores(0)
    for kv in range(n_kv):
        first = kv * GQA_GROUP
        s_loc, s_ctx = ahead
        if kv + 1 < n_kv:
            ahead = scores(kv + 1)
        vl = v_ref[kv, pl.ds(start, span), :]
        sink = jnp.zeros((1, width), F32)
        for grp in range(GQA_GROUP):
            sink = jnp.where(head_of_col == grp, sink_ref[first + grp] * LOG2_E, sink)
        m = jnp.maximum(jnp.maximum(jnp.max(s_loc, axis=0, keepdims=True),
                                    jnp.max(s_ctx, axis=0, keepdims=True)), sink)
        p_loc = jnp.exp2(s_loc - m)
        p_ctx = jnp.exp2(s_ctx - m)
        den = (jnp.sum(p_loc, axis=0, keepdims=True) + jnp.sum(p_ctx, axis=0, keepdims=True)
               + jnp.exp2(sink - m))
        out_t = (_dot_tn(vl, p_loc.astype(BF16)) + _dot_tn(vc_ref[kv], p_ctx.astype(BF16))) / den
        for grp in range(GQA_GROUP):
            hd = first + grp
            heads_t[hd * HEAD_DIM:(hd + 1) * HEAD_DIM, :] = (
                out_t[:, grp * blk:(grp + 1) * blk].astype(BF16))
    y = _dot_tn(heads_t[...], wo_ref[...])
    o_ref[...] = x_ref[...] + gt_ref[...] * y


def _attn_lat(q, k, v, kc, vc, sink, x, mod, w_o):
    b, l, d = x.shape
    n_heads, n_kv, past = q.shape[1], k.shape[1], kc.shape[2]
    assert l % ATTN_BLK == 0 and l >= 3 * ATTN_BLK
    blk = pl.BlockSpec((None, ATTN_BLK, d), lambda i, j: (i, j, 0))
    full = lambda n: pl.BlockSpec((None, n_kv, n, HEAD_DIM), lambda i, j: (i, 0, 0, 0))
    return pl.pallas_call(
        _attn_lat_body,
        out_shape=jax.ShapeDtypeStruct(x.shape, F32),
        grid=(b, l // ATTN_BLK),
        in_specs=[pl.BlockSpec(memory_space=pltpu.SMEM),
                  pl.BlockSpec((None, n_heads, ATTN_BLK, HEAD_DIM), lambda i, j: (i, 0, j, 0)),
                  full(l), full(l), full(past), full(past), blk,
                  _mod_block(d, 5, True), _const_spec(w_o.shape)],
        out_specs=blk,
        scratch_shapes=[pltpu.VMEM((d, ATTN_BLK), BF16),
                        pltpu.VMEM((3 * ATTN_BLK, ATTN_BLK), F32)],
        compiler_params=_params("parallel", "arbitrary"),
        name="attn_lat",
    )(sink, q, k, v, kc, vc, x, mod, w_o)


def _segment_apply(z, mat_bf16):
    return jnp.concatenate(
        [_dot2_exact_rhs(z[:, p * LANES:(p + 1) * LANES], mat_bf16)
         for p in range(z.shape[1] // LANES)], axis=1)


def _rwkv_pre_body(x_ref, xp_ref, xn_ref, sh_ref, sc_ref, g_ref, mu_ref, wrkv_ref,
                   w0_ref, w1_ref, w2_ref, a0_ref, a1_ref, a2_ref, g1_ref, g2_ref,
                   kk_scale_ref, ka_ref,
                   r_ref, v_ref, kk_ref, lw_ref, kd_ref, a_ref, gg_ref):
    t = pl.program_id(1)
    nt = pl.num_programs(1)
    g = g_ref[...]
    sh = sh_ref[...]
    sc = sc_ref[...]
    h = _mod_norm(x_ref[...], g, sh, sc)
    tm = h.shape[0]
    halo = xp_ref.shape[0]
    h_before = _mod_norm(xp_ref[...], g, sh, sc)[halo - 1:halo]
    h_after = _mod_norm(xn_ref[...], g, sh, sc)[0:1]
    h_before = jnp.where(t == 0, 0.0, h_before)
    h_after = jnp.where(t == nt - 1, 0.0, h_after)
    row = lax.broadcasted_iota(jnp.int32, h.shape, 0)
    prev = jnp.where(row == 0, h_before, pltpu.roll(h, 1, 0))
    nxt = jnp.where(row == tm - 1, h_after, pltpu.roll(h, tm - 1, 0))
    xx = 0.5 * (prev + nxt) - h
    mix = lambda i: (h + xx * mu_ref[i:i + 1, :]).astype(BF16)
    r = _dot(mix(0), wrkv_ref[0])
    xw = mix(1)
    k = _dot(mix(2), wrkv_ref[1])
    v = _dot(mix(3), wrkv_ref[2])
    xa = mix(4)
    xg = mix(5)
    r_ref[...] = r
    v_ref[...] = v
    kk = k * kk_scale_ref[...]
    ssq = _segment_apply(kk * kk, _head_ones(1.0))
    kk_ref[...] = kk * lax.rsqrt(ssq + KK_EPS)
    ka = ka_ref[...]
    for d in range(2):
        lora = _dot(jnp.tanh(_dot(xw, w1_ref[d])).astype(BF16), w2_ref[d])
        lw_ref[d] = -DECAY_SCALE * _sigmoid(w0_ref[d:d + 1, :] + lora)
        a = _sigmoid(a0_ref[d:d + 1, :] + _dot(_dot(xa, a1_ref[d]).astype(BF16), a2_ref[d]))
        a_ref[d] = a
        kd_ref[d] = k * (1.0 + (a - 1.0) * ka)
        gg_ref[d] = _dot(_sigmoid(_dot(xg, g1_ref[d])).astype(BF16), g2_ref[d])


def _rwkv_pre(x, mod, per_batch, g, p, tm):
    b, l, d = x.shape
    halo = 8
    nh = l // halo
    tok = pl.BlockSpec((None, tm, d), lambda i, t: (i, t, 0))
    tok2 = pl.BlockSpec((2, None, tm, d), lambda i, t: (0, i, t, 0))
    before = pl.BlockSpec((None, halo, d),
                          lambda i, t: (i, jnp.maximum(t * (tm // halo) - 1, 0), 0))
    after = pl.BlockSpec((None, halo, d),
                         lambda i, t: (i, jnp.minimum((t + 1) * (tm // halo), nh - 1), 0))
    one = jax.ShapeDtypeStruct((b, l, d), F32)
    two = jax.ShapeDtypeStruct((2, b, l, d), F32)
    consts = [g.reshape(1, d), p["mu"], p["w_rkv"], p["w0"], p["w1"], p["w2"], p["a0"],
              p["a1"], p["a2"], p["g1"], p["g2"], p["k_k"].reshape(1, d),
              p["k_a"].reshape(1, d)]
    return pl.pallas_call(
        _rwkv_pre_body,
        out_shape=(one, one, one, two, two, two, two),
        grid=(b, l // tm),
        in_specs=[tok, before, after, _mod_block(d, 3, per_batch), _mod_block(d, 4, per_batch)]
                 + [_const_spec(c.shape) for c in consts],
        out_specs=(tok, tok, tok, tok2, tok2, tok2, tok2),
        compiler_params=_params("parallel", "parallel"),
        name="rwkv_pre",
    )(x, x, x, mod, mod, *consts)


def _stack_heads(x, first_head):
    return jnp.concatenate([jnp.where(first_head, x, 0.0), jnp.where(first_head, 0.0, x)],
                           axis=0)


def _wkv_masks(chunk, reverse):
    rows = 2 * chunk
    ti = lax.broadcasted_iota(jnp.int32, (chunk, chunk), 0)
    si = lax.broadcasted_iota(jnp.int32, (chunk, chunk), 1)
    tri = jnp.where((si >= ti) if reverse else (si <= ti), 1.0, 0.0).astype(BF16)
    tt = lax.broadcasted_iota(jnp.int32, (rows, rows), 0)
    ss = lax.broadcasted_iota(jnp.int32, (rows, rows), 1)
    return dict(
        tri3=jnp.concatenate([tri, tri, tri], axis=1),
        first_head=lax.broadcasted_iota(jnp.int32, (chunk, LANES), 1) < HEAD_DIM,
        strict=(ss % chunk > tt % chunk) if reverse else (ss % chunk < tt % chunk),
        incl=(ss % chunk >= tt % chunk) if reverse else (ss % chunk <= tt % chunk),
        eye=jnp.where(tt == ss, 1.0, 0.0),
        avg=_head_ones(1.0 / HEAD_DIM),
        ones=_head_ones(1.0))


def _wkv_chain(r, lw, k, v, kk, ag, get_state, put_state, ln_g, ln_b, r_k, reverse, c):
    chunk = r.shape[0]
    rows = 2 * chunk
    cat = lambda *parts: jnp.concatenate(parts, axis=0)
    stack = lambda x: _stack_heads(x, c["first_head"])

    hi = lw.astype(BF16)
    mid = (lw - hi.astype(F32)).astype(BF16)
    lo = (lw - hi.astype(F32) - mid.astype(F32)).astype(BF16)
    cum = _dot(c["tri3"], cat(hi, mid, lo))
    yield
    total = cum[0:1] if reverse else cum[chunk - 1:chunk]
    b_vec = kk * ag
    grow = jnp.exp(-cum)
    to_end = jnp.exp(total - cum)
    lhs = cat(stack(-kk * jnp.exp(cum - lw)), stack(r * jnp.exp(cum)))
    rhs = cat(stack(b_vec * grow), stack(k * grow))
    rhs_end = cat(stack(b_vec * to_end), stack(k * to_end))
    vb = stack(v)
    gram = _dot1_nt(lhs, rhs)
    yield
    l_ab = jnp.where(c["strict"], gram[:rows, :rows], 0.0)
    l_ak = jnp.where(c["strict"], gram[:rows, rows:], 0.0)
    m_rb = jnp.where(c["incl"], gram[rows:, :rows], 0.0)
    m_rk = jnp.where(c["incl"], gram[rows:, rows:], 0.0)
    inv = c["eye"] + l_ab
    power = _dot1(l_ab, l_ab)
    yield
    span = 2
    while 2 * span < chunk:
        both = _dot1(power, jnp.concatenate([inv, power], axis=1))
        yield
        inv = inv + both[:, :rows]
        power = both[:, rows:]
        span *= 2
    inv = inv + _dot1(power, inv)
    yield
    s_prev = get_state()
    drive = _dot1(jnp.concatenate([lhs, cat(l_ak, m_rk)], axis=1), cat(s_prev, vb))
    yield
    u = _dot1(inv, drive[:rows])
    yield
    y_stacked = drive[rows:] + _dot1(m_rb, u)
    decay_rows = jnp.transpose(jnp.broadcast_to(jnp.exp(total), (LANES, LANES)))
    put_state(s_prev * decay_rows + _dot2_tn(rhs_end, cat(u, vb)))
    yield
    y = y_stacked[:chunk] + y_stacked[chunk:]
    cen = y - _dot2_exact_rhs(y, c["avg"])
    var = _dot2_exact_rhs(cen * cen, c["avg"])
    bonus = _dot2_exact_rhs(r * k * r_k, c["ones"]) * v
    yield
    return cen * lax.rsqrt(var + GN_EPS) * ln_g + ln_b + bonus


def _wkv_body(r_ref, lw_ref, k_ref, v_ref, kk_ref, ag_ref, s0_ref, lng_ref, lnb_ref, rk_ref,
              z_ref, sout_ref, state, *, reverse):
    c = pl.program_id(2)
    nc = pl.num_programs(2)
    group = state.shape[0]
    n_sub = r_ref.shape[0] // SCAN_CHUNK

    @pl.when(c == 0)
    def _():
        state[...] = s0_ref[...]

    consts = _wkv_masks(SCAN_CHUNK, reverse)
    order = list(range(n_sub))[::-1] if reverse else list(range(n_sub))
    carried = [[None] * (n_sub + 1) for _ in range(group)]
    chains, starts, where = [], [], []
    for step, sub in enumerate(order):
        rows = slice(sub * SCAN_CHUNK, (sub + 1) * SCAN_CHUNK)
        for i in range(group):
            ls = slice(i * LANES, (i + 1) * LANES)
            get = ((lambda i=i: state[i]) if step == 0 else
                   (lambda i=i, step=step: carried[i][step]))
            put = lambda s, i=i, step=step: carried[i].__setitem__(step + 1, s)
            chains.append(_wkv_chain(r_ref[rows, ls], lw_ref[rows, ls], k_ref[rows, ls],
                                     v_ref[rows, ls], kk_ref[rows, ls], ag_ref[rows, ls], get, put,
                                     lng_ref[:, ls], lnb_ref[:, ls], rk_ref[:, ls], reverse, consts))
            starts.append(step * SCAN_LAG)
            where.append((rows, ls))
    done = [None] * len(chains)
    tick = 0
    while any(d is None for d in done):
        for n, chain in enumerate(chains):
            if done[n] is None and tick >= starts[n]:
                try:
                    next(chain)
                except StopIteration as stop:
                    done[n] = stop.value
        tick += 1
    for n, (rows, ls) in enumerate(where):
        z_ref[rows, ls] = done[n]
    for i in range(group):
        state[i] = carried[i][n_sub]

    @pl.when(c == nc - 1)
    def _():
        for i in range(group):
            sout_ref[i] = carried[i][n_sub]


def _wkv_scan(r, lw, kd, v, kk, ag, s0, ln_g, ln_b, r_k, direction):
    b, l, d = r.shape
    reverse = direction == 1
    group = min(SCAN_PAIRS, d // LANES)
    width = group * LANES
    step_rows = SCAN_SUB * SCAN_CHUNK
    nc = l // step_rows
    at = (lambda c: nc - 1 - c) if reverse else (lambda c: c)
    tok = pl.BlockSpec((None, step_rows, width), lambda i, p, c: (i, at(c), p))
    tok2 = pl.BlockSpec((None, None, step_rows, width), lambda i, p, c: (direction, i, at(c), p))
    st = pl.BlockSpec((None, group, LANES, LANES), lambda i, p, c: (i, p, 0, 0))
    vec = pl.BlockSpec((1, width), lambda i, p, c: (0, p))
    return pl.pallas_call(
        functools.partial(_wkv_body, reverse=reverse),
        out_shape=(jax.ShapeDtypeStruct((b, l, d), F32),
                   jax.ShapeDtypeStruct(s0.shape, F32)),
        grid=(b, d // width, nc),
        in_specs=[tok, tok2, tok2, tok, tok, tok2, st, vec, vec, vec],
        out_specs=(tok, st),
        scratch_shapes=[pltpu.VMEM((group, LANES, LANES), F32)],
        compiler_params=_params("parallel", "parallel", "arbitrary"),
        name="wkv_rev" if reverse else "wkv_fwd",
    )(r, lw, kd, v, kk, ag, s0, ln_g.reshape(1, d), ln_b.reshape(1, d), r_k.reshape(1, d))


def _pair_states(s):
    b, h, n, _ = s.shape
    s = s.reshape(b, h // 2, 2, n, n)
    z = jnp.zeros_like(s[:, :, 0])
    top = jnp.concatenate([s[:, :, 0], z], axis=-1)
    bot = jnp.concatenate([z, s[:, :, 1]], axis=-1)
    return jnp.concatenate([top, bot], axis=-2)


def _unpair_states(s):
    b, p, _, _ = s.shape
    n = HEAD_DIM
    return jnp.stack([s[:, :, :n, :n], s[:, :, n:, n:]], axis=2).reshape(b, 2 * p, n, n)


def _rwkv_post_body(zf_ref, zb_ref, gg_ref, x_ref, gt_ref, wo_ref, o_ref):
    y = (zf_ref[...] * gg_ref[0] + zb_ref[...] * gg_ref[1]).astype(BF16)
    o_ref[...] = x_ref[...] + gt_ref[...] * _dot(y, wo_ref[...])


def _rwkv_post(zf, zb, gg, x, mod, per_batch, w_o, tm):
    b, l, d = x.shape
    tok = pl.BlockSpec((None, tm, d), lambda i, t: (i, t, 0))
    tok2 = pl.BlockSpec((2, None, tm, d), lambda i, t: (0, i, t, 0))
    return pl.pallas_call(
        _rwkv_post_body,
        out_shape=jax.ShapeDtypeStruct(x.shape, F32),
        grid=(b, l // tm),
        in_specs=[tok, tok, tok2, tok, _mod_block(d, 5, per_batch), _const_spec(w_o.shape)],
        out_specs=tok,
        compiler_params=_params("parallel", "parallel"),
        name="rwkv_post",
    )(zf, zb, gg, x, mod, w_o)


def _token_tile(l):
    return 256 if l % 256 == 0 else 128


def kernel(x_prompt, x_sample, cache_k, cache_v, state_wkv, c, c_ctx, ada_w, ada_b, norm_g, ffn_w_in, ffn_w_out, attn_w_qkv, attn_w_o, attn_sink, rwkv_mu, rwkv_w_rkv, rwkv_w0, rwkv_w1, rwkv_w2, rwkv_a0, rwkv_a1, rwkv_a2, rwkv_g1, rwkv_g2, rwkv_k_k, rwkv_k_a, rwkv_r_k, rwkv_ln_g, rwkv_ln_b, rwkv_w_o, norm_f):
    depth = ada_w.shape[0]
    bp, sp, d = x_prompt.shape
    bs, ls, _ = x_sample.shape
    n_heads = d // HEAD_DIM
    kv_dim = (n_heads // GQA_GROUP) * HEAD_DIM
    tm_p = _token_tile(sp)
    tm_s = _token_tile(ls)

    rows = -(-(1 + bs) // 8) * 8
    cond = jnp.zeros((rows, d), F32).at[0].set(c_ctx).at[1:1 + bs].set(c)
    mods = _ada(cond, ada_w, ada_b)

    ffn_w_in_b = ffn_w_in.astype(BF16)
    ffn_w_out_b = ffn_w_out.astype(BF16)
    rope_tables = _rope_tables(ls, GRID_W)

    xp, xs = x_prompt, x_sample
    ctx_k, ctx_v, ctx_state = [], [], []
    for l in range(depth):
        mod_p = mods[l, 0:1].reshape(1, N_ADA, 1, d)
        mod_s = mods[l, 1:1 + bs].reshape(bs, N_ADA, 1, d)
        xp = _ffn(xp, mod_p, (0, 1, 2), False, norm_g[l, 0], ffn_w_in_b[l, 0], ffn_w_out_b[l, 0])
        xs = _ffn(xs, mod_s, (0, 1, 2), True, norm_g[l, 0], ffn_w_in_b[l, 0], ffn_w_out_b[l, 0])
        i = l // 2
        if l % 2 == 0:
            w_qkv = attn_w_qkv[i].astype(BF16)
            w_o = attn_w_o[i].astype(BF16)
            q, k, v = _qkv(xp, mod_p, False, norm_g[l, 1], w_qkv, None, tm_p)
            ctx_k.append(k.reshape(bp, sp, kv_dim // HEAD_DIM, HEAD_DIM))
            ctx_v.append(v.reshape(bp, sp, kv_dim // HEAD_DIM, HEAD_DIM))
            xp = _attn_ctx(q, k, v, attn_sink[i], xp, mod_p, w_o)
            q, k, v = _qkv(xs, mod_s, True, norm_g[l, 1], w_qkv, rope_tables, tm_s)
            head_major = lambda t: jnp.swapaxes(t, 1, 2).astype(BF16)
            xs = _attn_lat(q, k, v, head_major(cache_k[:, i]), head_major(cache_v[:, i]),
                           attn_sink[i], xs, mod_s, w_o)
        else:
            p = dict(mu=rwkv_mu[i], w_rkv=rwkv_w_rkv[i].astype(BF16), w0=rwkv_w0[i],
                     w1=rwkv_w1[i].astype(BF16), w2=rwkv_w2[i].astype(BF16), a0=rwkv_a0[i],
                     a1=rwkv_a1[i].astype(BF16), a2=rwkv_a2[i].astype(BF16),
                     g1=rwkv_g1[i].astype(BF16), g2=rwkv_g2[i].astype(BF16),
                     k_k=rwkv_k_k[i], k_a=rwkv_k_a[i])
            w_o = rwkv_w_o[i].astype(BF16)
            r_k = rwkv_r_k[i].reshape(2, d)
            zero_state = jnp.zeros((bp, n_heads // 2, LANES, LANES), F32)
            new_states = []
            for x, mod, per_batch, tm, is_prompt in ((xp, mod_p, False, tm_p, True),
                                                     (xs, mod_s, True, tm_s, False)):
                r, v, kk, lw, kd, ag, gg = _rwkv_pre(x, mod, per_batch, norm_g[l, 1], p, tm)
                zs = []
                for dr in range(2):
                    s0 = (zero_state if is_prompt else
                          _pair_states(jnp.swapaxes(state_wkv[:, i, dr], -1, -2)))
                    z, s_end = _wkv_scan(r, lw, kd, v, kk, ag, s0,
                                         rwkv_ln_g[i, dr], rwkv_ln_b[i, dr], r_k[dr], dr)
                    zs.append(z)
                    if is_prompt:
                        new_states.append(jnp.swapaxes(_unpair_states(s_end), -1, -2))
                x = _rwkv_post(zs[0], zs[1], gg, x, mod, per_batch, w_o, tm)
                if is_prompt:
                    xp = x
                else:
                    xs = x
            ctx_state.append(jnp.stack(new_states, axis=1))
        final_g = norm_f if l == depth - 1 else None
        xp = _ffn(xp, mod_p, (6, 7, 8), False, norm_g[l, 2], ffn_w_in_b[l, 1], ffn_w_out_b[l, 1],
                  final_g)
        xs = _ffn(xs, mod_s, (6, 7, 8), True, norm_g[l, 2], ffn_w_in_b[l, 1], ffn_w_out_b[l, 1],
                  final_g)

    return (xp, xs, jnp.stack(ctx_k, axis=1), jnp.stack(ctx_v, axis=1),
            jnp.stack(ctx_state, axis=1))
```

```python
import functools

import jax
import jax.numpy as jnp
from jax import lax
from jax.experimental import pallas as pl
from jax.experimental.pallas import tpu as pltpu

F32 = jnp.float32
BF16 = jnp.bfloat16

HEAD_DIM = 64
GQA_GROUP = 4
LANES = 128
N_ADA = 9
ATTN_BLK = 128
GRID_W = 64
ROPE_BASE = 10000.0
ROPE_PAIRS = HEAD_DIM // 4
RMS_EPS = 1e-6
GN_EPS = 64e-5
KK_EPS = 1e-12
DECAY_SCALE = 0.6065306597126334
NEG_INF = -1e30
LOG2_E = 1.4426950408889634
FFN_TILE = 512
SCAN_CHUNK = 64
SCAN_PAIRS = 8
SCAN_SUB = 4
SCAN_LAG = 3
VMEM_LIMIT = 56 * 1024 * 1024


def _params(*sem):
    return pltpu.CompilerParams(dimension_semantics=sem, vmem_limit_bytes=VMEM_LIMIT)


def _const_spec(shape):
    nd = len(shape)
    return pl.BlockSpec(shape, lambda *_: (0,) * nd, pipeline_mode=pl.Buffered(1))


def _mod_block(d_model, chunk, per_batch):
    def index(b, *_):
        return (b if per_batch else 0, chunk, 0, 0)
    return pl.BlockSpec((None, None, 1, d_model), index)


def _mod_norm(x, g, shift, scale):
    ms = jnp.mean(x * x, axis=-1, keepdims=True)
    return (x * lax.rsqrt(ms + RMS_EPS) * g) * (1.0 + scale) + shift


def _sigmoid(x):
    return 0.5 * jnp.tanh(0.5 * x) + 0.5


def _dot(a, b):
    return jnp.dot(a, b, preferred_element_type=F32)


def _dot_nt(a, b):
    return lax.dot_general(a, b, (((1,), (1,)), ((), ())), preferred_element_type=F32)


def _dot_tn(a, b):
    return lax.dot_general(a, b, (((0,), (0,)), ((), ())), preferred_element_type=F32)


def _split(a):
    hi = a.astype(BF16)
    lo = (a - hi.astype(F32)).astype(BF16)
    return hi, lo


def _dot1(a, b):
    return _dot(a.astype(BF16), b.astype(BF16))


def _dot1_nt(a, b):
    return _dot_nt(a.astype(BF16), b.astype(BF16))


def _dot2_tn(a, b):
    ah, al = _split(a)
    bh = b.astype(BF16)
    return _dot_tn(ah, bh) + _dot_tn(al, bh)


def _dot2_exact_rhs(a, b_bf16):
    ah, al = _split(a)
    return _dot(jnp.concatenate([ah, al], axis=1), jnp.concatenate([b_bf16, b_bf16], axis=0))


def _head_ones(scale):
    r = lax.broadcasted_iota(jnp.int32, (LANES, LANES), 0) // HEAD_DIM
    c = lax.broadcasted_iota(jnp.int32, (LANES, LANES), 1) // HEAD_DIM
    return jnp.where(r == c, scale, 0.0).astype(BF16)


def _ada_body(c_ref, w_ref, b_ref, o_ref):
    c = c_ref[...]
    s = (c * jax.nn.sigmoid(c)).astype(BF16)
    o_ref[...] = _dot(s, w_ref[...].astype(BF16)) + b_ref[...]


def _ada(cond, ada_w, ada_b):
    depth, d_model, n_out = ada_w.shape
    rows = cond.shape[0]
    tn = d_model
    return pl.pallas_call(
        _ada_body,
        out_shape=jax.ShapeDtypeStruct((depth, rows, n_out), F32),
        grid=(depth, n_out // tn),
        in_specs=[
            pl.BlockSpec((rows, d_model), lambda l, n: (0, 0)),
            pl.BlockSpec((None, d_model, tn), lambda l, n: (l, 0, n)),
            pl.BlockSpec((None, 1, tn), lambda l, n: (l, 0, n)),
        ],
        out_specs=pl.BlockSpec((None, rows, tn), lambda l, n: (l, 0, n)),
        compiler_params=_params("parallel", "parallel"),
        name="ada",
    )(cond, ada_w, ada_b.reshape(depth, 1, n_out))


def _ffn_body(x_ref, sh_ref, sc_ref, gt_ref, g_ref, win_ref, wout_ref, *rest):
    final_g_ref, o_ref = rest if len(rest) == 2 else (None, rest[0])
    x = x_ref[...]
    h = _mod_norm(x, g_ref[...], sh_ref[...], sc_ref[...]).astype(BF16)
    hh = _dot(h, win_ref[...])
    d_ff = hh.shape[1] // 2
    gate = hh[:, :d_ff]
    act = (gate * jax.nn.sigmoid(gate) * hh[:, d_ff:]).astype(BF16)
    y = _dot(act, wout_ref[...])
    out = x + (0.5 * gt_ref[...]) * y
    if final_g_ref is not None:
        ms = jnp.mean(out * out, axis=-1, keepdims=True)
        out = out * lax.rsqrt(ms + RMS_EPS) * final_g_ref[...]
    o_ref[...] = out


def _ffn(x, mod, chunks, per_batch, g, w_in, w_out, final_g=None):
    shape = x.shape
    d = shape[-1]
    if not per_batch:
        x = x.reshape(1, -1, d)
    b, l, _ = x.shape
    tm = FFN_TILE if l % FFN_TILE == 0 else _token_tile(l)
    tok = pl.BlockSpec((None, tm, d), lambda i, t: (i, t, 0))
    extra = [] if final_g is None else [final_g.reshape(1, d)]
    return pl.pallas_call(
        _ffn_body,
        out_shape=jax.ShapeDtypeStruct(x.shape, F32),
        grid=(b, l // tm),
        in_specs=[
            tok,
            _mod_block(d, chunks[0], per_batch),
            _mod_block(d, chunks[1], per_batch),
            _mod_block(d, chunks[2], per_batch),
            _const_spec((1, d)),
            _const_spec(w_in.shape),
            _const_spec(w_out.shape),
        ] + [_const_spec((1, d))] * len(extra),
        out_specs=tok,
        compiler_params=_params("parallel", "parallel"),
        name="ffn",
    )(x, mod, mod, mod, g.reshape(1, d), w_in, w_out, *extra).reshape(shape)


def _rope(x, cos, sin_signed):
    width = x.shape[1]
    lane = lax.broadcasted_iota(jnp.int32, x.shape, 1)
    first_half = (lane % (2 * ROPE_PAIRS)) < ROPE_PAIRS
    partner = jnp.where(first_half,
                        pltpu.roll(x, width - ROPE_PAIRS, 1),
                        pltpu.roll(x, ROPE_PAIRS, 1))
    return x * cos + partner * sin_signed


def _qkv_body(*refs, rope, d_model, kv_dim):
    if rope:
        x_ref, sh_ref, sc_ref, g_ref, w_ref, cos_ref, sin_ref, q_ref, k_ref, v_ref = refs
    else:
        x_ref, sh_ref, sc_ref, g_ref, w_ref, q_ref, k_ref, v_ref = refs
    h = _mod_norm(x_ref[...], g_ref[...], sh_ref[...], sc_ref[...]).astype(BF16)
    qkv = _dot(h, w_ref[...])
    q = qkv[:, :d_model]
    k = qkv[:, d_model:d_model + kv_dim]
    v = qkv[:, d_model + kv_dim:]
    if rope:
        cos = cos_ref[...]
        sin = sin_ref[...]
        q = _rope(q, jnp.concatenate([cos] * (d_model // LANES), axis=1),
                  jnp.concatenate([sin] * (d_model // LANES), axis=1))
        k = _rope(k, jnp.concatenate([cos] * (kv_dim // LANES), axis=1),
                  jnp.concatenate([sin] * (kv_dim // LANES), axis=1))
    if rope:
        q = (q * (HEAD_DIM ** -0.5 * LOG2_E)).astype(BF16)
        for hd in range(d_model // HEAD_DIM):
            q_ref[hd] = q[:, hd * HEAD_DIM:(hd + 1) * HEAD_DIM]
        for hd in range(kv_dim // HEAD_DIM):
            k_ref[hd] = k[:, hd * HEAD_DIM:(hd + 1) * HEAD_DIM].astype(BF16)
            v_ref[hd] = v[:, hd * HEAD_DIM:(hd + 1) * HEAD_DIM].astype(BF16)
    else:
        q_ref[...] = (q * (HEAD_DIM ** -0.5)).astype(BF16)
        k_ref[...] = k
        v_ref[...] = v


def _qkv(x, mod, per_batch, g, w_qkv, rope_tables, tm):
    b, l, d = x.shape
    kv_dim = (w_qkv.shape[1] - d) // 2
    rope = rope_tables is not None
    tok = lambda w: pl.BlockSpec((None, tm, w), lambda i, t: (i, t, 0))
    heads = lambda w: pl.BlockSpec((None, w // HEAD_DIM, tm, HEAD_DIM), lambda i, t: (i, 0, t, 0))
    head_shape = lambda w: jax.ShapeDtypeStruct((b, w // HEAD_DIM, l, HEAD_DIM), BF16)
    in_specs = [tok(d), _mod_block(d, 3, per_batch), _mod_block(d, 4, per_batch),
                _const_spec((1, d)), _const_spec(w_qkv.shape)]
    args = [x, mod, mod, g.reshape(1, d), w_qkv]
    if rope:
        tab = pl.BlockSpec((tm, LANES), lambda i, t: (t, 0))
        in_specs += [tab, tab]
        args += list(rope_tables)
    return pl.pallas_call(
        functools.partial(_qkv_body, rope=rope, d_model=d, kv_dim=kv_dim),
        out_shape=((head_shape(d), head_shape(kv_dim), head_shape(kv_dim)) if rope else
                   (jax.ShapeDtypeStruct((b, l, d), BF16),
                    jax.ShapeDtypeStruct((b, l, kv_dim), F32),
                    jax.ShapeDtypeStruct((b, l, kv_dim), F32))),
        grid=(b, l // tm),
        in_specs=in_specs,
        out_specs=((heads(d), heads(kv_dim), heads(kv_dim)) if rope else
                   (tok(d), tok(kv_dim), tok(kv_dim))),
        compiler_params=_params("parallel", "parallel"),
        name="qkv_rope" if rope else "qkv",
    )(*args)


def _rope_tables(l, grid_w):
    pos = jnp.arange(l, dtype=jnp.int32)
    row = (pos // grid_w).astype(F32)
    col = (pos % grid_w).astype(F32)
    inv = jnp.power(ROPE_BASE, -jnp.arange(ROPE_PAIRS, dtype=F32) / ROPE_PAIRS)
    ang_r = row[:, None] * inv
    ang_c = col[:, None] * inv
    cos = jnp.concatenate([jnp.cos(ang_r)] * 2 + [jnp.cos(ang_c)] * 2, axis=1)
    sin = jnp.concatenate([-jnp.sin(ang_r), jnp.sin(ang_r),
                           -jnp.sin(ang_c), jnp.sin(ang_c)], axis=1)
    return jnp.concatenate([cos, cos], axis=1), jnp.concatenate([sin, sin], axis=1)


def _softmax_pv(scores, values, sink):
    m = jnp.maximum(functools.reduce(
        jnp.maximum, [jnp.max(s, axis=1, keepdims=True) for s in scores]), sink)
    den = jnp.exp(sink - m)
    acc = None
    for s, v in zip(scores, values):
        p = jnp.exp(s - m)
        den = den + jnp.sum(p, axis=1, keepdims=True)
        pv = _dot(p.astype(BF16), v)
        acc = pv if acc is None else acc + pv
    return acc / den


def _attn_ctx_body(sink_ref, q_ref, k_ref, v_ref, x_ref, gt_ref, wo_ref, o_ref, head_out):
    n_heads = q_ref.shape[1] // HEAD_DIM
    for kv in range(n_heads // GQA_GROUP):
        ks = slice(kv * HEAD_DIM, (kv + 1) * HEAD_DIM)
        kb = k_ref[:, ks].astype(BF16)
        vb = v_ref[:, ks].astype(BF16)
        for grp in range(GQA_GROUP):
            hd = kv * GQA_GROUP + grp
            hs = slice(hd * HEAD_DIM, (hd + 1) * HEAD_DIM)
            s = _dot_nt(q_ref[:, hs], kb)
            head_out[:, hs] = _softmax_pv([s], [vb], sink_ref[hd]).astype(BF16)
    y = _dot(head_out[...], wo_ref[...])
    o_ref[...] = x_ref[...] + gt_ref[...] * y


def _attn_ctx(q, k, v, sink, x, mod, w_o):
    b, s, d = x.shape
    kv_dim = k.shape[2]
    tok = lambda w: pl.BlockSpec((None, s, w), lambda i: (i, 0, 0))
    return pl.pallas_call(
        _attn_ctx_body,
        out_shape=jax.ShapeDtypeStruct(x.shape, F32),
        grid=(b,),
        in_specs=[pl.BlockSpec(memory_space=pltpu.SMEM),
                  tok(d), tok(kv_dim), tok(kv_dim), tok(d),
                  _mod_block(d, 5, False), _const_spec(w_o.shape)],
        out_specs=tok(d),
        scratch_shapes=[pltpu.VMEM((s, d), BF16)],
        compiler_params=_params("parallel"),
        name="attn_ctx",
    )(sink, q, k, v, x, mod, w_o)


def _attn_lat_body(sink_ref, q_ref, k_ref, v_ref, kc_ref, vc_ref, x_ref, gt_ref, wo_ref,
                   o_ref, heads_t, bias):
    n_heads, blk, _ = q_ref.shape
    seq = k_ref.shape[1]
    span = 3 * blk
    width = GQA_GROUP * blk
    j = pl.program_id(1)
    start = pl.multiple_of(jnp.clip((j - 1) * blk, 0, seq - span), blk)
    kpos = start + lax.broadcasted_iota(jnp.int32, (span, blk), 0)
    qpos = j * blk + lax.broadcasted_iota(jnp.int32, (span, blk), 1)
    bias[...] = jnp.where(jnp.abs(qpos - kpos) <= blk, 0.0, NEG_INF)
    head_of_col = lax.broadcasted_iota(jnp.int32, (1, width), 1) // blk
    n_kv = n_heads // GQA_GROUP

    def scores(kv):
        q4 = q_ref[kv * GQA_GROUP:(kv + 1) * GQA_GROUP].reshape(width, HEAD_DIM)
        kl = k_ref[kv, pl.ds(start, span), :]
        s_loc = _dot_nt(kl, q4) + jnp.concatenate([bias[...]] * GQA_GROUP, axis=1)
        return s_loc, _dot_nt(kc_ref[kv], q4)

    ahead = scores(0)
    for kv in range(n_kv):
        first = kv * GQA_GROUP
        s_loc, s_ctx = ahead
        if kv + 1 < n_kv:
            ahead = scores(kv + 1)
        vl = v_ref[kv, pl.ds(start, span), :]
        sink = jnp.zeros((1, width), F32)
        for grp in range(GQA_GROUP):
            sink = jnp.where(head_of_col == grp, sink_ref[first + grp] * LOG2_E, sink)
        m = jnp.maximum(jnp.maximum(jnp.max(s_loc, axis=0, keepdims=True),
                                    jnp.max(s_ctx, axis=0, keepdims=True)), sink)
        p_loc = jnp.exp2(s_loc - m)
        p_ctx = jnp.exp2(s_ctx - m)
        den = (jnp.sum(p_loc, axis=0, keepdims=True) + jnp.sum(p_ctx, axis=0, keepdims=True)
               + jnp.exp2(sink - m))
        out_t = (_dot_tn(vl, p_loc.astype(BF16)) + _dot_tn(vc_ref[kv], p_ctx.astype(BF16))) / den
        for grp in range(GQA_GROUP):
            hd = first + grp
            heads_t[hd * HEAD_DIM:(hd + 1) * HEAD_DIM, :] = (
                out_t[:, grp * blk:(grp + 1) * blk].astype(BF16))
    y = _dot_tn(heads_t[...], wo_ref[...])
    o_ref[...] = x_ref[...] + gt_ref[...] * y


def _attn_lat(q, k, v, kc, vc, sink, x, mod, w_o):
    b, l, d = x.shape
    n_heads, n_kv, past = q.shape[1], k.shape[1], kc.shape[2]
    assert l % ATTN_BLK == 0 and l >= 3 * ATTN_BLK
    blk = pl.BlockSpec((None, ATTN_BLK, d), lambda i, j: (i, j, 0))
    full = lambda n: pl.BlockSpec((None, n_kv, n, HEAD_DIM), lambda i, j: (i, 0, 0, 0))
    return pl.pallas_call(
        _attn_lat_body,
        out_shape=jax.ShapeDtypeStruct(x.shape, F32),
        grid=(b, l // ATTN_BLK),
        in_specs=[pl.BlockSpec(memory_space=pltpu.SMEM),
                  pl.BlockSpec((None, n_heads, ATTN_BLK, HEAD_DIM), lambda i, j: (i, 0, j, 0)),
                  full(l), full(l), full(past), full(past), blk,
                  _mod_block(d, 5, True), _const_spec(w_o.shape)],
        out_specs=blk,
        scratch_shapes=[pltpu.VMEM((d, ATTN_BLK), BF16),
                        pltpu.VMEM((3 * ATTN_BLK, ATTN_BLK), F32)],
        compiler_params=_params("parallel", "arbitrary"),
        name="attn_lat",
    )(sink, q, k, v, kc, vc, x, mod, w_o)


def _segment_apply(z, mat_bf16):
    return jnp.concatenate(
        [_dot2_exact_rhs(z[:, p * LANES:(p + 1) * LANES], mat_bf16)
         for p in range(z.shape[1] // LANES)], axis=1)


def _rwkv_pre_body(x_ref, xp_ref, xn_ref, sh_ref, sc_ref, g_ref, mu_ref, wrkv_ref,
                   w0_ref, w1_ref, w2_ref, a0_ref, a1_ref, a2_ref, g1_ref, g2_ref,
                   kk_scale_ref, ka_ref,
                   r_ref, v_ref, kk_ref, lw_ref, kd_ref, a_ref, gg_ref):
    t = pl.program_id(1)
    nt = pl.num_programs(1)
    g = g_ref[...]
    sh = sh_ref[...]
    sc = sc_ref[...]
    h = _mod_norm(x_ref[...], g, sh, sc)
    tm = h.shape[0]
    halo = xp_ref.shape[0]
    h_before = _mod_norm(xp_ref[...], g, sh, sc)[halo - 1:halo]
    h_after = _mod_norm(xn_ref[...], g, sh, sc)[0:1]
    h_before = jnp.where(t == 0, 0.0, h_before)
    h_after = jnp.where(t == nt - 1, 0.0, h_after)
    row = lax.broadcasted_iota(jnp.int32, h.shape, 0)
    prev = jnp.where(row == 0, h_before, pltpu.roll(h, 1, 0))
    nxt = jnp.where(row == tm - 1, h_after, pltpu.roll(h, tm - 1, 0))
    xx = 0.5 * (prev + nxt) - h
    mix = lambda i: (h + xx * mu_ref[i:i + 1, :]).astype(BF16)
    r = _dot(mix(0), wrkv_ref[0])
    xw = mix(1)
    k = _dot(mix(2), wrkv_ref[1])
    v = _dot(mix(3), wrkv_ref[2])
    xa = mix(4)
    xg = mix(5)
    r_ref[...] = r.astype(BF16)
    v_ref[...] = v.astype(BF16)
    kk = k * kk_scale_ref[...]
    ssq = _segment_apply(kk * kk, _head_ones(1.0))
    kk_ref[...] = (kk * lax.rsqrt(ssq + KK_EPS)).astype(BF16)
    ka = ka_ref[...]
    for d in range(2):
        lora = _dot(jnp.tanh(_dot(xw, w1_ref[d])).astype(BF16), w2_ref[d])
        lw_ref[d] = -DECAY_SCALE * _sigmoid(w0_ref[d:d + 1, :] + lora)
        a = _sigmoid(a0_ref[d:d + 1, :] + _dot(_dot(xa, a1_ref[d]).astype(BF16), a2_ref[d]))
        a_ref[d] = a.astype(BF16)
        kd_ref[d] = (k * (1.0 + (a - 1.0) * ka)).astype(BF16)
        gg_ref[d] = _dot(_sigmoid(_dot(xg, g1_ref[d])).astype(BF16), g2_ref[d]).astype(BF16)


def _rwkv_pre(x, mod, per_batch, g, p, tm):
    b, l, d = x.shape
    halo = 8
    nh = l // halo
    tok = pl.BlockSpec((None, tm, d), lambda i, t: (i, t, 0))
    tok2 = pl.BlockSpec((2, None, tm, d), lambda i, t: (0, i, t, 0))
    before = pl.BlockSpec((None, halo, d),
                          lambda i, t: (i, jnp.maximum(t * (tm // halo) - 1, 0), 0))
    after = pl.BlockSpec((None, halo, d),
                         lambda i, t: (i, jnp.minimum((t + 1) * (tm // halo), nh - 1), 0))
    one = jax.ShapeDtypeStruct((b, l, d), BF16)
    two = jax.ShapeDtypeStruct((2, b, l, d), BF16)
    two_f32 = jax.ShapeDtypeStruct((2, b, l, d), F32)
    consts = [g.reshape(1, d), p["mu"], p["w_rkv"], p["w0"], p["w1"], p["w2"], p["a0"],
              p["a1"], p["a2"], p["g1"], p["g2"], p["k_k"].reshape(1, d),
              p["k_a"].reshape(1, d)]
    return pl.pallas_call(
        _rwkv_pre_body,
        out_shape=(one, one, one, two_f32, two, two, two),
        grid=(b, l // tm),
        in_specs=[tok, before, after, _mod_block(d, 3, per_batch), _mod_block(d, 4, per_batch)]
                 + [_const_spec(c.shape) for c in consts],
        out_specs=(tok, tok, tok, tok2, tok2, tok2, tok2),
        compiler_params=_params("parallel", "parallel"),
        name="rwkv_pre",
    )(x, x, x, mod, mod, *consts)


def _stack_heads(x, first_head):
    return jnp.concatenate([jnp.where(first_head, x, 0.0), jnp.where(first_head, 0.0, x)],
                           axis=0)


def _wkv_masks(chunk, reverse):
    rows = 2 * chunk
    ti = lax.broadcasted_iota(jnp.int32, (chunk, chunk), 0)
    si = lax.broadcasted_iota(jnp.int32, (chunk, chunk), 1)
    tri = jnp.where((si >= ti) if reverse else (si <= ti), 1.0, 0.0).astype(BF16)
    tt = lax.broadcasted_iota(jnp.int32, (rows, rows), 0)
    ss = lax.broadcasted_iota(jnp.int32, (rows, rows), 1)
    return dict(
        tri3=jnp.concatenate([tri, tri, tri], axis=1),
        first_head=lax.broadcasted_iota(jnp.int32, (chunk, LANES), 1) < HEAD_DIM,
        strict=(ss % chunk > tt % chunk) if reverse else (ss % chunk < tt % chunk),
        incl=(ss % chunk >= tt % chunk) if reverse else (ss % chunk <= tt % chunk),
        eye=jnp.where(tt == ss, 1.0, 0.0),
        avg=_head_ones(1.0 / HEAD_DIM),
        ones=_head_ones(1.0))


def _wkv_chain(r, lw, k, v, kk, ag, gate, get_state, put_state, ln_g, ln_b, r_k, reverse, c):
    chunk = r.shape[0]
    rows = 2 * chunk
    cat = lambda *parts: jnp.concatenate(parts, axis=0)
    stack = lambda x: _stack_heads(x, c["first_head"])

    hi = lw.astype(BF16)
    mid = (lw - hi.astype(F32)).astype(BF16)
    lo = (lw - hi.astype(F32) - mid.astype(F32)).astype(BF16)
    cum = _dot(c["tri3"], cat(hi, mid, lo))
    yield
    total = cum[0:1] if reverse else cum[chunk - 1:chunk]
    b_vec = kk * ag
    grow = jnp.exp(-cum)
    to_end = jnp.exp(total - cum)
    lhs = cat(stack(-kk * jnp.exp(cum - lw)), stack(r * jnp.exp(cum)))
    rhs = cat(stack(b_vec * grow), stack(k * grow))
    rhs_end = cat(stack(b_vec * to_end), stack(k * to_end))
    vb = stack(v)
    gram = _dot1_nt(lhs, rhs)
    yield
    l_ab = jnp.where(c["strict"], gram[:rows, :rows], 0.0)
    l_ak = jnp.where(c["strict"], gram[:rows, rows:], 0.0)
    m_rb = jnp.where(c["incl"], gram[rows:, :rows], 0.0)
    m_rk = jnp.where(c["incl"], gram[rows:, rows:], 0.0)
    inv = c["eye"] + l_ab
    power = _dot1(l_ab, l_ab)
    yield
    span = 2
    while 2 * span < chunk:
        both = _dot1(power, jnp.concatenate([inv, power], axis=1))
        yield
        inv = inv + both[:, :rows]
        power = both[:, rows:]
        span *= 2
    inv = inv + _dot1(power, inv)
    yield
    s_prev = get_state()
    drive = _dot1(jnp.concatenate([lhs, cat(l_ak, m_rk)], axis=1), cat(s_prev, vb))
    yield
    u = _dot1(inv, drive[:rows])
    yield
    y_stacked = drive[rows:] + _dot1(m_rb, u)
    decay_rows = jnp.transpose(jnp.broadcast_to(jnp.exp(total), (LANES, LANES)))
    put_state(s_prev * decay_rows + _dot2_tn(rhs_end, cat(u, vb)))
    yield
    y = y_stacked[:chunk] + y_stacked[chunk:]
    cen = y - _dot2_exact_rhs(y, c["avg"])
    var = _dot2_exact_rhs(cen * cen, c["avg"])
    bonus = _dot2_exact_rhs(r * k * r_k, c["ones"]) * v
    yield
    return (cen * lax.rsqrt(var + GN_EPS) * ln_g + ln_b + bonus) * gate


def _wkv_body(r_ref, lw_ref, k_ref, v_ref, kk_ref, ag_ref, gg_ref, s0_ref, lng_ref, lnb_ref, rk_ref,
              z_ref, sout_ref, state, *, reverse):
    c = pl.program_id(2)
    nc = pl.num_programs(2)
    group = state.shape[0]
    n_sub = r_ref.shape[0] // SCAN_CHUNK

    @pl.when(c == 0)
    def _():
        state[...] = s0_ref[...]

    consts = _wkv_masks(SCAN_CHUNK, reverse)
    order = list(range(n_sub))[::-1] if reverse else list(range(n_sub))
    carried = [[None] * (n_sub + 1) for _ in range(group)]
    chains, starts, where = [], [], []
    for step, sub in enumerate(order):
        rows = slice(sub * SCAN_CHUNK, (sub + 1) * SCAN_CHUNK)
        for i in range(group):
            ls = slice(i * LANES, (i + 1) * LANES)
            get = ((lambda i=i: state[i]) if step == 0 else
                   (lambda i=i, step=step: carried[i][step]))
            put = lambda s, i=i, step=step: carried[i].__setitem__(step + 1, s)
            f32 = lambda ref: ref[rows, ls].astype(F32)
            chains.append(_wkv_chain(f32(r_ref), lw_ref[rows, ls], f32(k_ref),
                                     f32(v_ref), f32(kk_ref), f32(ag_ref), f32(gg_ref),
                                     get, put,
                                     lng_ref[:, ls], lnb_ref[:, ls], rk_ref[:, ls], reverse, consts))
            starts.append(step * SCAN_LAG)
            where.append((rows, ls))
    done = [None] * len(chains)
    tick = 0
    while any(d is None for d in done):
        for n, chain in enumerate(chains):
            if done[n] is None and tick >= starts[n]:
                try:
                    next(chain)
                except StopIteration as stop:
                    done[n] = stop.value
        tick += 1
    for n, (rows, ls) in enumerate(where):
        z_ref[rows, ls] = done[n]
    for i in range(group):
        state[i] = carried[i][n_sub]

    @pl.when(c == nc - 1)
    def _():
        for i in range(group):
            sout_ref[i] = carried[i][n_sub]


def _wkv_scan(r, lw, kd, v, kk, ag, gg, s0, ln_g, ln_b, r_k, direction):
    b, l, d = r.shape
    reverse = direction == 1
    group = min(SCAN_PAIRS, d // LANES)
    width = group * LANES
    step_rows = SCAN_SUB * SCAN_CHUNK
    nc = l // step_rows
    at = (lambda c: nc - 1 - c) if reverse else (lambda c: c)
    tok = pl.BlockSpec((None, step_rows, width), lambda i, p, c: (i, at(c), p))
    tok2 = pl.BlockSpec((None, None, step_rows, width), lambda i, p, c: (direction, i, at(c), p))
    st = pl.BlockSpec((None, group, LANES, LANES), lambda i, p, c: (i, p, 0, 0))
    vec = pl.BlockSpec((1, width), lambda i, p, c: (0, p))
    return pl.pallas_call(
        functools.partial(_wkv_body, reverse=reverse),
        out_shape=(jax.ShapeDtypeStruct((b, l, d), F32),
                   jax.ShapeDtypeStruct(s0.shape, F32)),
        grid=(b, d // width, nc),
        in_specs=[tok, tok2, tok2, tok, tok, tok2, tok2, st, vec, vec, vec],
        out_specs=(tok, st),
        scratch_shapes=[pltpu.VMEM((group, LANES, LANES), F32)],
        compiler_params=_params("parallel", "parallel", "arbitrary"),
        name="wkv_rev" if reverse else "wkv_fwd",
    )(r, lw, kd, v, kk, ag, gg, s0, ln_g.reshape(1, d), ln_b.reshape(1, d), r_k.reshape(1, d))


def _pair_states(s):
    b, h, n, _ = s.shape
    s = s.reshape(b, h // 2, 2, n, n)
    z = jnp.zeros_like(s[:, :, 0])
    top = jnp.concatenate([s[:, :, 0], z], axis=-1)
    bot = jnp.concatenate([z, s[:, :, 1]], axis=-1)
    return jnp.concatenate([top, bot], axis=-2)


def _unpair_states(s):
    b, p, _, _ = s.shape
    n = HEAD_DIM
    return jnp.stack([s[:, :, :n, :n], s[:, :, n:, n:]], axis=2).reshape(b, 2 * p, n, n)


def _rwkv_post_body(zf_ref, zb_ref, x_ref, gt_ref, wo_ref, o_ref):
    y = (zf_ref[...] + zb_ref[...]).astype(BF16)
    o_ref[...] = x_ref[...] + gt_ref[...] * _dot(y, wo_ref[...])


def _rwkv_post(zf, zb, x, mod, per_batch, w_o, tm):
    b, l, d = x.shape
    tok = pl.BlockSpec((None, tm, d), lambda i, t: (i, t, 0))
    return pl.pallas_call(
        _rwkv_post_body,
        out_shape=jax.ShapeDtypeStruct(x.shape, F32),
        grid=(b, l // tm),
        in_specs=[tok, tok, tok, _mod_block(d, 5, per_batch), _const_spec(w_o.shape)],
        out_specs=tok,
        compiler_params=_params("parallel", "parallel"),
        name="rwkv_post",
    )(zf, zb, x, mod, w_o)


def _token_tile(l):
    return 256 if l % 256 == 0 else 128


def kernel(x_prompt, x_sample, cache_k, cache_v, state_wkv, c, c_ctx, ada_w, ada_b, norm_g, ffn_w_in, ffn_w_out, attn_w_qkv, attn_w_o, attn_sink, rwkv_mu, rwkv_w_rkv, rwkv_w0, rwkv_w1, rwkv_w2, rwkv_a0, rwkv_a1, rwkv_a2, rwkv_g1, rwkv_g2, rwkv_k_k, rwkv_k_a, rwkv_r_k, rwkv_ln_g, rwkv_ln_b, rwkv_w_o, norm_f):
    depth = ada_w.shape[0]
    bp, sp, d = x_prompt.shape
    bs, ls, _ = x_sample.shape
    n_heads = d // HEAD_DIM
    kv_dim = (n_heads // GQA_GROUP) * HEAD_DIM
    tm_p = _token_tile(sp)
    tm_s = _token_tile(ls)

    rows = -(-(1 + bs) // 8) * 8
    cond = jnp.zeros((rows, d), F32).at[0].set(c_ctx).at[1:1 + bs].set(c)
    mods = _ada(cond, ada_w, ada_b)

    ffn_w_in_b = ffn_w_in.astype(BF16)
    ffn_w_out_b = ffn_w_out.astype(BF16)
    rope_tables = _rope_tables(ls, GRID_W)

    xp, xs = x_prompt, x_sample
    ctx_k, ctx_v, ctx_state = [], [], []
    for l in range(depth):
        mod_p = mods[l, 0:1].reshape(1, N_ADA, 1, d)
        mod_s = mods[l, 1:1 + bs].reshape(bs, N_ADA, 1, d)
        xp = _ffn(xp, mod_p, (0, 1, 2), False, norm_g[l, 0], ffn_w_in_b[l, 0], ffn_w_out_b[l, 0])
        xs = _ffn(xs, mod_s, (0, 1, 2), True, norm_g[l, 0], ffn_w_in_b[l, 0], ffn_w_out_b[l, 0])
        i = l // 2
        if l % 2 == 0:
            w_qkv = attn_w_qkv[i].astype(BF16)
            w_o = attn_w_o[i].astype(BF16)
            q, k, v = _qkv(xp, mod_p, False, norm_g[l, 1], w_qkv, None, tm_p)
            ctx_k.append(k.reshape(bp, sp, kv_dim // HEAD_DIM, HEAD_DIM))
            ctx_v.append(v.reshape(bp, sp, kv_dim // HEAD_DIM, HEAD_DIM))
            xp = _attn_ctx(q, k, v, attn_sink[i], xp, mod_p, w_o)
            q, k, v = _qkv(xs, mod_s, True, norm_g[l, 1], w_qkv, rope_tables, tm_s)
            head_major = lambda t: jnp.swapaxes(t, 1, 2).astype(BF16)
            xs = _attn_lat(q, k, v, head_major(cache_k[:, i]), head_major(cache_v[:, i]),
                           attn_sink[i], xs, mod_s, w_o)
        else:
            p = dict(mu=rwkv_mu[i], w_rkv=rwkv_w_rkv[i].astype(BF16), w0=rwkv_w0[i],
                     w1=rwkv_w1[i].astype(BF16), w2=rwkv_w2[i].astype(BF16), a0=rwkv_a0[i],
                     a1=rwkv_a1[i].astype(BF16), a2=rwkv_a2[i].astype(BF16),
                     g1=rwkv_g1[i].astype(BF16), g2=rwkv_g2[i].astype(BF16),
                     k_k=rwkv_k_k[i], k_a=rwkv_k_a[i])
            w_o = rwkv_w_o[i].astype(BF16)
            r_k = rwkv_r_k[i].reshape(2, d)
            zero_state = jnp.zeros((bp, n_heads // 2, LANES, LANES), F32)
            new_states = []
            for x, mod, per_batch, tm, is_prompt in ((xp, mod_p, False, tm_p, True),
                                                     (xs, mod_s, True, tm_s, False)):
                r, v, kk, lw, kd, ag, gg = _rwkv_pre(x, mod, per_batch, norm_g[l, 1], p, tm)
                zs = []
                for dr in range(2):
                    s0 = (zero_state if is_prompt else
                          _pair_states(jnp.swapaxes(state_wkv[:, i, dr], -1, -2)))
                    z, s_end = _wkv_scan(r, lw, kd, v, kk, ag, gg, s0,
                                         rwkv_ln_g[i, dr], rwkv_ln_b[i, dr], r_k[dr], dr)
                    zs.append(z)
                    if is_prompt:
                        new_states.append(jnp.swapaxes(_unpair_states(s_end), -1, -2))
                x = _rwkv_post(zs[0], zs[1], x, mod, per_batch, w_o, tm)
                if is_prompt:
                    xp = x
                else:
                    xs = x
            ctx_state.append(jnp.stack(new_states, axis=1))
        final_g = norm_f if l == depth - 1 else None
        xp = _ffn(xp, mod_p, (6, 7, 8), False, norm_g[l, 2], ffn_w_in_b[l, 1], ffn_w_out_b[l, 1],
                  final_g)
        xs = _ffn(xs, mod_s, (6, 7, 8), True, norm_g[l, 2], ffn_w_in_b[l, 1], ffn_w_out_b[l, 1],
                  final_g)

    return (xp, xs, jnp.stack(ctx_k, axis=1), jnp.stack(ctx_v, axis=1),
            jnp.stack(ctx_state, axis=1))
```
